```python
import functools
import jax, jax.numpy as jnp
from jax import lax
import numpy as np

D_MODEL = 1024
BATCH = 8
SEQ = 2048
DEPTH = 2
DEC_BATCH = 32
DEC_SEQ = 16
PAST_LEN = 1024

CHUNK = 64
Q_BLOCK = 128
DSA_Q_BLOCK = 64
EPS = 1e-6
NEG_INF = -1e30
ROPE_THETA = 10000.0

H_A = 8
DH_A = 64
H_B = 8
DH_B = 64
H_IDX = 8
D_IDX = 64
TOPK_MAX = 256
H_C = 16
Q_LORA = 384
KV_LORA = 128
NOPE = 64
ROPE_DIM = 32
DV = 64

N_AB = (DEPTH + 1) // 2
N_C = DEPTH // 2

AB_COLS = (H_A * DH_A, H_A * DH_A, H_A * DH_A, H_A, H_A * DH_A,
           H_B * DH_B, H_B * DH_B, H_B * DH_B, H_IDX * D_IDX, D_IDX, H_IDX, H_B * DH_B)
AB_IN = sum(AB_COLS)
AB_MIX = H_A * DH_A + H_B * DH_B
C_COLS = (Q_LORA, KV_LORA, ROPE_DIM, H_C * DV)
C_IN = sum(C_COLS)
C_MIX = H_C * DV

kernel_name = 'fox_dsa_mla_streaming_encoder_step'


def rms_norm(x, g):
    xf = x.astype(jnp.float32)
    y = xf * lax.rsqrt(jnp.mean(xf * xf, axis=-1, keepdims=True) + EPS)
    return (y * g.astype(jnp.float32)).astype(x.dtype)


def apply_rope(x, pos):
    half = x.shape[-1] // 2
    inv_freq = ROPE_THETA ** (-jnp.arange(half, dtype=jnp.float32) / half)
    ang = pos.astype(jnp.float32)[:, None] * inv_freq[None, :]
    cos = jnp.cos(ang)[None, :, None, :]
    sin = jnp.sin(ang)[None, :, None, :]
    xf = x.astype(jnp.float32)
    x1, x2 = xf[..., :half], xf[..., half:]
    return jnp.concatenate([x1 * cos - x2 * sin, x2 * cos + x1 * sin], axis=-1).astype(x.dtype)


def split_cols(z, sizes):
    out, start = [], 0
    for s in sizes:
        out.append(z[..., start:start + s])
        start += s
    return out


def sweep_query_blocks(fn, q_args, block):
    tq = q_args[0].shape[1]
    if tq <= block:
        return fn(*q_args)
    nb = tq // block
    blocked = tuple(jnp.moveaxis(a.reshape((a.shape[0], nb, block) + a.shape[2:]), 1, 0) for a in q_args)
    out = lax.map(lambda xs: fn(*xs), blocked)
    out = jnp.moveaxis(out, 0, 1)
    return out.reshape((out.shape[0], tq) + out.shape[3:])


def fox_block(q, fq, qpos, k, v, fk, kpos):
    qpos = qpos[0]
    s = jnp.einsum('bqhd,bkhd->bhqk', q, k).astype(jnp.float32) * (DH_A ** -0.5)
    s = s + jnp.swapaxes(fq, 1, 2)[..., None] - jnp.swapaxes(fk, 1, 2)[:, :, None, :]
    mask = kpos[None, :] <= qpos[:, None]
    p = jax.nn.softmax(jnp.where(mask, s, NEG_INF), axis=-1)
    return jnp.einsum('bhqk,bkhd->bqhd', p.astype(v.dtype), v)


def dsa_block(q, iq, iw, qpos, kv, ik, kpos, n_sel):
    qpos = qpos[0]
    rel = jax.nn.relu(jnp.einsum('bqhd,bkd->bqhk', iq, ik).astype(jnp.float32))
    score = jnp.einsum('bqh,bqhk->bqk', iw.astype(jnp.float32), rel)
    adm = (kpos[None, :] // CHUNK) <= (qpos[:, None] // CHUNK)
    score = jnp.where(adm[None], score, NEG_INF)
    _, sel = lax.top_k(score, n_sel)
    valid = (kpos[sel] // CHUNK) <= (qpos[None, :, None] // CHUNK)
    kv_sel = jax.vmap(lambda a, i: a[i])(kv, sel)
    k_sel, v_sel = kv_sel[..., :DH_B], kv_sel[..., DH_B:]
    s = jnp.einsum('bqhd,bqkhd->bhqk', q, k_sel).astype(jnp.float32) * (DH_B ** -0.5)
    p = jax.nn.softmax(jnp.where(valid[:, None], s, NEG_INF), axis=-1)
    return jnp.einsum('bhqk,bqkhd->bqhd', p.astype(v_sel.dtype), v_sel)


def mla_block(q_lat, q_rope, qpos, ckv, krope, kpos):
    qpos = qpos[0]
    s = (jnp.einsum('bqhc,bkc->bhqk', q_lat, ckv)
         + jnp.einsum('bqhr,bkr->bhqk', q_rope, krope)).astype(jnp.float32) * ((NOPE + ROPE_DIM) ** -0.5)
    mask = (kpos[None, :] // CHUNK) <= (qpos[:, None] // CHUNK)
    p = jax.nn.softmax(jnp.where(mask, s, NEG_INF), axis=-1)
    return jnp.einsum('bhqk,bkc->bqhc', p.astype(ckv.dtype), ckv)


def with_past(cache, new_rows):
    if cache is None:
        return new_rows
    return tuple(jnp.concatenate([c.astype(n.dtype), n], axis=1) for c, n in zip(cache, new_rows))


def mixer_ab(h, pos, w_in, f_bias, w_out, cache):
    b, t, _ = h.shape
    qa, ka, va, fa, ga, qb, kb, vb, iq, ik, iw, gb = split_cols(h @ w_in, AB_COLS)
    qa = qa.reshape(b, t, H_A, DH_A)
    ka = ka.reshape(b, t, H_A, DH_A)
    va = va.reshape(b, t, H_A, DH_A)
    logf = jax.nn.log_sigmoid(fa.astype(jnp.float32) + f_bias.astype(jnp.float32))
    qb = apply_rope(qb.reshape(b, t, H_B, DH_B), pos)
    kb = apply_rope(kb.reshape(b, t, H_B, DH_B), pos)
    vb = vb.reshape(b, t, H_B, DH_B)
    iq = apply_rope(iq.reshape(b, t, H_IDX, D_IDX), pos)
    ik = apply_rope(ik[:, :, None, :], pos)[:, :, 0]
    iw = iw * (H_IDX ** -0.5 * D_IDX ** -0.5)
    new_rows = (ka, va, logf, kb, vb, ik)
    ka_all, va_all, logf_all, kb_all, vb_all, ik_all = with_past(cache, new_rows)
    n_keys = ka_all.shape[1]
    kpos = jnp.arange(n_keys)
    qpos = pos[None]
    cum_f = jnp.cumsum(logf_all, axis=1)
    oa = sweep_query_blocks(
        functools.partial(fox_block, k=ka_all, v=va_all, fk=cum_f, kpos=kpos),
        (qa, cum_f[:, n_keys - t:], qpos), Q_BLOCK)
    n_sel = min(TOPK_MAX, n_keys // 4)
    kv_b = jnp.concatenate([kb_all, vb_all], axis=-1)
    ob = sweep_query_blocks(
        functools.partial(dsa_block, kv=kv_b, ik=ik_all, kpos=kpos, n_sel=n_sel),
        (qb, iq, iw, qpos), DSA_Q_BLOCK)
    mixed = jnp.concatenate([oa.reshape(b, t, H_A * DH_A) * jax.nn.silu(ga),
                             ob.reshape(b, t, H_B * DH_B) * jax.nn.silu(gb)], axis=-1)
    return mixed @ w_out, new_rows


def mixer_c(h, pos, w_in, q_norm_g, kv_norm_g, w_uq, w_uk, w_uv, w_out, cache):
    b, t, _ = h.shape
    cq, ckv, kr, g = split_cols(h @ w_in, C_COLS)
    q = (rms_norm(cq, q_norm_g) @ w_uq).reshape(b, t, H_C, NOPE + ROPE_DIM)
    q_nope = q[..., :NOPE]
    q_rope = apply_rope(q[..., NOPE:], pos)
    q_lat = jnp.einsum('bthn,chn->bthc', q_nope, w_uk)
    ckv = rms_norm(ckv, kv_norm_g)
    kr = apply_rope(kr[:, :, None, :], pos)[:, :, 0]
    new_rows = (ckv, kr)
    ckv_all, kr_all = with_past(cache, new_rows)
    kpos = jnp.arange(ckv_all.shape[1])
    o_lat = sweep_query_blocks(
        functools.partial(mla_block, ckv=ckv_all, krope=kr_all, kpos=kpos),
        (q_lat, q_rope, pos[None]), Q_BLOCK)
    o = jnp.einsum('bthc,chd->bthd', o_lat, w_uv).reshape(b, t, C_MIX)
    return (o * jax.nn.silu(g)) @ w_out, new_rows


def trunk(x, c, caches, w):
    t = x.shape[1]
    past = 0 if caches is None else caches[0].shape[2]
    pos = past + jnp.arange(t)
    ab_states, c_states = [], []
    for layer in range(DEPTH):
        mod = jax.nn.silu(c) @ w['ada_w'][layer] + w['ada_b'][layer]
        shift, scale, gate = jnp.split(mod, 3, axis=-1)
        h = rms_norm(x, w['norm_g'][layer]) * (1 + scale[:, None]) + shift[:, None]
        i = layer // 2
        if layer % 2 == 0:
            cache = None if caches is None else tuple(a[i] for a in caches[:6])
            y, st = mixer_ab(h, pos, w['ab_w_in'][i], w['ab_f_bias'][i], w['ab_w_out'][i], cache)
            ab_states.append(st)
        else:
            cache = None if caches is None else tuple(a[i] for a in caches[6:])
            y, st = mixer_c(h, pos, w['c_w_in'][i], w['c_q_norm_g'][i], w['c_kv_norm_g'][i],
                            w['c_w_uq'][i], w['c_w_uk'][i], w['c_w_uv'][i], w['c_w_out'][i], cache)
            c_states.append(st)
        x = x + gate[:, None] * y
    x = rms_norm(x, w['final_norm_g'])
    new_state = [jnp.stack(z) for z in zip(*ab_states)] + [jnp.stack(z) for z in zip(*c_states)]
    return x, new_state


def setup_inputs(seed: int = 0) -> dict:
    key = jax.random.key(seed)
    ks = jax.random.split(key, 32)
    nrm = jax.random.normal
    f32 = jnp.float32
    d = D_MODEL
    return {
        'x_prompt': nrm(ks[0], (BATCH, SEQ, d), f32),
        'x_sample': nrm(ks[1], (DEC_BATCH, DEC_SEQ, d), f32),
        'c_prompt': nrm(ks[2], (BATCH, d), f32),
        'c_sample': nrm(ks[3], (DEC_BATCH, d), f32),
        'cache_a_k': nrm(ks[4], (N_AB, DEC_BATCH, PAST_LEN, H_A, DH_A), f32),
        'cache_a_v': nrm(ks[5], (N_AB, DEC_BATCH, PAST_LEN, H_A, DH_A), f32),
        'cache_a_logf': jax.nn.log_sigmoid(2.0 + 0.5 * nrm(ks[6], (N_AB, DEC_BATCH, PAST_LEN, H_A), f32)),
        'cache_b_k': nrm(ks[7], (N_AB, DEC_BATCH, PAST_LEN, H_B, DH_B), f32),
        'cache_b_v': nrm(ks[8], (N_AB, DEC_BATCH, PAST_LEN, H_B, DH_B), f32),
        'cache_b_idx_k': nrm(ks[9], (N_AB, DEC_BATCH, PAST_LEN, D_IDX), f32),
        'cache_c_latent': nrm(ks[10], (N_C, DEC_BATCH, PAST_LEN, KV_LORA), f32),
        'cache_c_krope': nrm(ks[11], (N_C, DEC_BATCH, PAST_LEN, ROPE_DIM), f32),
        'norm_g': 1.0 + 0.02 * nrm(ks[12], (DEPTH, d), f32),
        'ada_w': nrm(ks[13], (DEPTH, d, 3 * d), f32) * d ** -0.5,
        'ada_b': 0.02 * nrm(ks[14], (DEPTH, 3 * d), f32),
        'final_norm_g': 1.0 + 0.02 * nrm(ks[15], (d,), f32),
        'ab_w_in': nrm(ks[16], (N_AB, d, AB_IN), f32) * d ** -0.5,
        'ab_f_bias': 2.0 + 0.5 * nrm(ks[17], (N_AB, H_A), f32),
        'ab_w_out': nrm(ks[18], (N_AB, AB_MIX, d), f32) * AB_MIX ** -0.5,
        'c_w_in': nrm(ks[19], (N_C, d, C_IN), f32) * d ** -0.5,
        'c_q_norm_g': 1.0 + 0.02 * nrm(ks[20], (N_C, Q_LORA), f32),
        'c_kv_norm_g': 1.0 + 0.02 * nrm(ks[21], (N_C, KV_LORA), f32),
        'c_w_uq': nrm(ks[22], (N_C, Q_LORA, H_C * (NOPE + ROPE_DIM)), f32) * Q_LORA ** -0.5,
        'c_w_uk': nrm(ks[23], (N_C, KV_LORA, H_C, NOPE), f32) * KV_LORA ** -0.5,
        'c_w_uv': nrm(ks[24], (N_C, KV_LORA, H_C, DV), f32) * KV_LORA ** -0.5,
        'c_w_out': nrm(ks[25], (N_C, C_MIX, d), f32) * C_MIX ** -0.5,
    }


def reference(x_prompt, x_sample, c_prompt, c_sample,
              cache_a_k, cache_a_v, cache_a_logf, cache_b_k, cache_b_v, cache_b_idx_k,
              cache_c_latent, cache_c_krope,
              norm_g, ada_w, ada_b, final_norm_g,
              ab_w_in, ab_f_bias, ab_w_out,
              c_w_in, c_q_norm_g, c_kv_norm_g, c_w_uq, c_w_uk, c_w_uv, c_w_out):
    w = dict(norm_g=norm_g, ada_w=ada_w, ada_b=ada_b, final_norm_g=final_norm_g,
             ab_w_in=ab_w_in, ab_f_bias=ab_f_bias, ab_w_out=ab_w_out,
             c_w_in=c_w_in, c_q_norm_g=c_q_norm_g, c_kv_norm_g=c_kv_norm_g,
             c_w_uq=c_w_uq, c_w_uk=c_w_uk, c_w_uv=c_w_uv, c_w_out=c_w_out)
    caches = (cache_a_k, cache_a_v, cache_a_logf, cache_b_k, cache_b_v, cache_b_idx_k,
              cache_c_latent, cache_c_krope)
    y_prompt, p_state = trunk(x_prompt, c_prompt, None, w)
    p_a_k, p_a_v, p_a_logf, p_b_k, p_b_v, p_b_idx_k, p_c_latent, p_c_krope = p_state
    y_sample, s_state = trunk(x_sample, c_sample, caches, w)
    s_a_k, s_a_v, s_a_logf, s_b_k, s_b_v, s_b_idx_k, s_c_latent, s_c_krope = s_state
    return (y_prompt, y_sample,
            p_a_k, p_a_v, p_a_logf, p_b_k, p_b_v, p_b_idx_k, p_c_latent, p_c_krope,
            s_a_k, s_a_v, s_a_logf, s_b_k, s_b_v, s_b_idx_k, s_c_latent, s_c_krope)
```

```python
import functools
import struct

import jax
import jax.numpy as jnp
from jax import lax
from jax.experimental import pallas as pl
from jax.experimental.pallas import tpu as pltpu

F32 = jnp.float32
BF16 = jnp.bfloat16
I32 = jnp.int32

D_MODEL = 1024
CHUNK = 64
CHUNK_SHIFT = 6
EPS = 1e-6
NEG_INF = -1e30
ROPE_THETA = 10000.0
H_A = 8
H_B = 8
H_IDX = 8
DH = 64
TOPK_MAX = 256
H_C = 16
Q_LORA = 384
KV_LORA = 128
NOPE = 64
ROPE_DIM = 32
DV = 64
HW = H_A * DH

LANES = 128
VMEM_LIMIT = 56 * 1024 * 1024

QA, KA, VA, GA, QB, KB, VB, IQ, GB, SM = (0, 512, 1024, 1536, 2048, 2560, 3072, 3584, 4096, 4608)
ZW = SM + LANES
SM_IK, SM_FA, SM_IW = 0, 64, 72
C_G, C_Q, C_KV, C_KR = 0, 1024, 1408, 1536
CW = C_KR + LANES

IW_SCALE = H_IDX ** -0.5 * DH ** -0.5
QK_SCALE = DH ** -0.5
C_SCALE = (NOPE + ROPE_DIM) ** -0.5
INT_MIN = -2 ** 31
_NEG_INF_BITS = struct.unpack("<i", struct.pack("<f", NEG_INF))[0]
KEY_MASKED = _NEG_INF_BITS ^ 0x7FFFFFFF


def _cparams(n_axes):
    return pltpu.CompilerParams(dimension_semantics=("arbitrary",) * n_axes, vmem_limit_bytes=VMEM_LIMIT)


def _sigmoid(x):
    return 1.0 / (1.0 + jnp.exp(-x))


def _silu(x):
    return x * _sigmoid(x)


def _lane_iota(shape):
    return lax.broadcasted_iota(I32, shape, len(shape) - 1)


def _rope(x, cos, sin, dim):
    half = dim // 2
    first = (_lane_iota(x.shape) & (dim - 1)) < half
    rot = jnp.where(first, pltpu.roll(x, LANES - half, 1), pltpu.roll(x, half, 1))
    return x * cos + rot * sin


def _rms(x, g):
    return x * lax.rsqrt(jnp.mean(x * x, axis=-1, keepdims=True) + EPS) * g


def _norm_mod(x, mod_ref, g, bt, tt):
    xn = _rms(x, g)
    shift = mod_ref[:, 0, :]
    scale = mod_ref[:, 1, :]
    if bt == 1:
        return xn * (1.0 + scale) + shift
    xn = xn.reshape(bt, tt, x.shape[-1])
    h = xn * (1.0 + scale)[:, None, :] + shift[:, None, :]
    return h.reshape(bt * tt, x.shape[-1])


def _gated_residual(x, y, mod_ref, bt, tt):
    gate = mod_ref[:, 2, :]
    if bt == 1:
        return x + gate * y
    d = x.shape[-1]
    return (x.reshape(bt, tt, d) + gate[:, None, :] * y.reshape(bt, tt, d)).reshape(bt * tt, d)


def _mod_kernel(c_ref, w_ref, b_ref, o_ref):
    s = _silu(c_ref[...]).astype(BF16)
    o_ref[0] = jnp.dot(s, w_ref[0].astype(BF16), preferred_element_type=F32) + b_ref[0]


def _mod_call(c_all, ada_w, ada_b):
    depth, d, n = ada_w.shape
    rows = c_all.shape[0]
    tn = 512
    return pl.pallas_call(
        _mod_kernel,
        out_shape=jax.ShapeDtypeStruct((depth, rows, n), F32),
        grid=(depth, n // tn),
        in_specs=[pl.BlockSpec((rows, d), lambda l, j: (0, 0)),
                  pl.BlockSpec((1, d, tn), lambda l, j: (l, 0, j)),
                  pl.BlockSpec((1, 1, tn), lambda l, j: (l, 0, j))],
        out_specs=pl.BlockSpec((1, rows, tn), lambda l, j: (l, 0, j)),
        compiler_params=_cparams(2),
        name="mod",
    )(c_all, ada_w, ada_b.reshape(depth, 1, n))


def _in_ab_kernel(x_ref, mod_ref, ng_ref, w_ref, cos_ref, sin_ref, fb_ref,
                  zb_ref, ka_ref, va_ref, kb_ref, vb_ref, sm_ref, *, bt, tt):
    h = _norm_mod(x_ref[...], mod_ref, ng_ref[...], bt, tt).astype(BF16)
    cos = cos_ref[...]
    sin = sin_ref[...]

    def proj(c0, n):
        return jnp.dot(h, w_ref[:, c0:c0 + n], preferred_element_type=F32)

    def rope_cols(z):
        return jnp.concatenate(
            [_rope(z[:, c:c + LANES], cos, sin, DH) for c in range(0, z.shape[1], LANES)], axis=1)

    zb_ref[:, QA:QA + HW] = (proj(QA, HW) * QK_SCALE).astype(BF16)
    z = proj(KA, HW)
    ka_ref[...] = z
    zb_ref[:, KA:KA + HW] = z.astype(BF16)
    z = proj(VA, HW)
    va_ref[...] = z
    zb_ref[:, VA:VA + HW] = z.astype(BF16)
    zb_ref[:, GA:GA + HW] = proj(GA, HW).astype(BF16)
    zb_ref[:, QB:QB + HW] = (rope_cols(proj(QB, HW)) * QK_SCALE).astype(BF16)
    z = rope_cols(proj(KB, HW))
    kb_ref[...] = z
    zb_ref[:, KB:KB + HW] = z.astype(BF16)
    z = proj(VB, HW)
    vb_ref[...] = z
    zb_ref[:, VB:VB + HW] = z.astype(BF16)
    zb_ref[:, IQ:IQ + HW] = rope_cols(proj(IQ, HW)).astype(BF16)
    zb_ref[:, GB:GB + HW] = proj(GB, HW).astype(BF16)

    z = proj(SM, LANES)
    lane = _lane_iota(z.shape)
    ik = _rope(z, cos, sin, DH)
    u = -(z + fb_ref[...])
    logf = -(jnp.maximum(u, 0.0) + jnp.log1p(jnp.exp(-jnp.abs(u))))
    sm_ref[...] = jnp.where(lane < SM_FA, ik, jnp.where(lane < SM_IW, logf, z * IW_SCALE))
    zb_ref[:, SM:SM + LANES] = jnp.where(lane < SM_FA, ik, pltpu.roll(ik, DH, 1)).astype(BF16)


def _in_ab_call(x2, mod, ng, w_ab, cos, sin, fb, *, t, bt, tt):
    n, d = x2.shape
    tm = bt * tt
    tpb = t // tt
    row = lambda i: (i, 0)
    return pl.pallas_call(
        functools.partial(_in_ab_kernel, bt=bt, tt=tt),
        out_shape=(jax.ShapeDtypeStruct((n, ZW), BF16),
                   jax.ShapeDtypeStruct((n, HW), F32), jax.ShapeDtypeStruct((n, HW), F32),
                   jax.ShapeDtypeStruct((n, HW), F32), jax.ShapeDtypeStruct((n, HW), F32),
                   jax.ShapeDtypeStruct((n, LANES), F32)),
        grid=(n // tm,),
        in_specs=[pl.BlockSpec((tm, d), row),
                  pl.BlockSpec((bt, 3, d), lambda i: (i * tt // t, 0, 0)),
                  pl.BlockSpec((1, d), lambda i: (0, 0)),
                  pl.BlockSpec((d, ZW), lambda i: (0, 0)),
                  pl.BlockSpec((tm, LANES), lambda i: (i % tpb, 0)),
                  pl.BlockSpec((tm, LANES), lambda i: (i % tpb, 0)),
                  pl.BlockSpec((1, LANES), lambda i: (0, 0))],
        out_specs=(pl.BlockSpec((tm, ZW), row),
                   pl.BlockSpec((tm, HW), row), pl.BlockSpec((tm, HW), row),
                   pl.BlockSpec((tm, HW), row), pl.BlockSpec((tm, HW), row),
                   pl.BlockSpec((tm, LANES), row)),
        compiler_params=_cparams(1),
        name="in_ab",
    )(x2, mod, ng, w_ab, cos, sin, fb)


def _cumsum_kernel(x_ref, o_ref):
    x = x_ref[0]
    lane = _lane_iota(x.shape)
    s = 1
    while s < x.shape[1]:
        x = x + jnp.where(lane >= s, pltpu.roll(x, s, 1), 0.0)
        s *= 2
    o_ref[0] = x


def _cumsum_call(logf_t):
    b, h, tk = logf_t.shape
    spec = pl.BlockSpec((1, h, tk), lambda i: (i, 0, 0))
    return pl.pallas_call(
        _cumsum_kernel, out_shape=jax.ShapeDtypeStruct(logf_t.shape, F32),
        grid=(b,), in_specs=[spec], out_specs=spec, compiler_params=_cparams(1), name="cumsum",
    )(logf_t)


def _online_update(s, v_blk, m, l, acc):
    m_new = jnp.maximum(m, jnp.max(s, axis=1, keepdims=True))
    alpha = jnp.exp(m - m_new)
    p = jnp.exp(s - m_new)
    l = alpha * l + jnp.sum(p, axis=1, keepdims=True)
    acc = alpha * acc + jnp.dot(p.astype(BF16), v_blk, preferred_element_type=F32)
    return m_new, l, acc


def _qk(q, k):
    return lax.dot_general(q, k, (((1,), (1,)), ((), ())), preferred_element_type=F32)


def _head_of_pair(x, hh):
    lane = _lane_iota(x.shape)
    keep = (lane < DH) if hh == 0 else (lane >= DH)
    return jnp.where(keep, x, jnp.zeros_like(x))


def _num_key_blocks(q_end, tk, nkb_max):
    return jnp.minimum((q_end + tk - 1) // tk, nkb_max)


def _fox_kernel(q_ref, k_ref, v_ref, g_ref, fq_ref, fk_ref, o_ref, *, tq, tk, past, nkb_max):
    q0 = past + pl.program_id(1) * tq
    nkb = _num_key_blocks(q0 + tq, tk, nkb_max)
    qpos = q0 + lax.broadcasted_iota(I32, (tq, 1), 0)
    kidx = lax.broadcasted_iota(I32, (1, tk), 1)
    lane = _lane_iota((tq, LANES))
    for p in range(H_A // 2):
        cols = slice(p * LANES, (p + 1) * LANES)
        qp = q_ref[0, :, cols]
        outs = []
        for hh in range(2):
            h = 2 * p + hh
            qh = _head_of_pair(qp, hh)
            fq = fq_ref[0, :, h:h + 1]

            def body(j, carry, qh=qh, fq=fq, h=h, cols=cols):
                k0 = pl.multiple_of(j * tk, tk)
                s = _qk(qh, k_ref[0, pl.ds(k0, tk), cols])
                s = s + fq - fk_ref[0, j, h:h + 1, :]
                s = jnp.where(k0 + kidx <= qpos, s, NEG_INF)
                return _online_update(s, v_ref[0, pl.ds(k0, tk), cols], *carry)

            m, l, acc = lax.fori_loop(
                0, nkb, body,
                (jnp.full((tq, 1), NEG_INF, F32), jnp.zeros((tq, 1), F32), jnp.zeros((tq, LANES), F32)))
            outs.append(acc / l)
        o = jnp.where(lane < DH, outs[0], outs[1])
        o_ref[0, :, cols] = (o * _silu(g_ref[0, :, cols].astype(F32))).astype(BF16)


def _fox_call(q_src, k_src, v_src, g_src, fq, fk, *, b, t, tk_total, tq, tk, past, q_col, k_col, v_col, g_col):
    nkb_max = tk_total // tk
    return pl.pallas_call(
        functools.partial(_fox_kernel, tq=tq, tk=tk, past=past, nkb_max=nkb_max),
        out_shape=jax.ShapeDtypeStruct((b, t, HW), BF16),
        grid=(b, t // tq),
        in_specs=[pl.BlockSpec((1, tq, HW), lambda i, j: (i, j, q_col)),
                  pl.BlockSpec((1, tk_total, HW), lambda i, j: (i, 0, k_col)),
                  pl.BlockSpec((1, tk_total, HW), lambda i, j: (i, 0, v_col)),
                  pl.BlockSpec((1, tq, HW), lambda i, j: (i, j, g_col)),
                  pl.BlockSpec((1, tq, H_A), lambda i, j: (i, j, 0)),
                  pl.BlockSpec((1, nkb_max, H_A, tk), lambda i, j: (i, 0, 0, 0))],
        out_specs=pl.BlockSpec((1, tq, HW), lambda i, j: (i, j, 0)),
        compiler_params=_cparams(2),
        name="fox",
    )(q_src, k_src, v_src, g_src, fq, fk)


def _dsa_kernel(q_ref, iq_ref, sm_ref, g_ref, k_ref, v_ref, ik_ref, o_ref, key_sc, bias_sc,
                *, tq, tk, past, n_valid, n_sel, nkb_max):
    q0 = past + pl.program_id(1) * tq
    nkb = _num_key_blocks(q0 + tq, tk, nkb_max)
    qchunk = (q0 + lax.broadcasted_iota(I32, (tq, 1), 0)) >> CHUNK_SHIFT
    kidx = lax.broadcasted_iota(I32, (1, tk), 1)
    lane = _lane_iota((tq, LANES))

    def score_body(j, _):
        k0 = pl.multiple_of(j * tk, tk)
        ikb = ik_ref[0, pl.ds(k0, tk), :]
        acc = jnp.zeros((tq, tk), F32)
        for p in range(H_IDX // 2):
            iqp = iq_ref[0, :, p * LANES:(p + 1) * LANES]
            for hh in range(2):
                h = 2 * p + hh
                x = _qk(_head_of_pair(iqp, hh), ikb)
                acc = acc + sm_ref[0, :, SM_IW + h:SM_IW + h + 1] * jnp.maximum(x, 0.0)
        kpos = k0 + kidx
        adm = ((kpos >> CHUNK_SHIFT) <= qchunk) & (kpos < n_valid)
        bits = lax.bitcast_convert_type(jnp.where(adm, acc, NEG_INF), I32)
        key = bits ^ ((bits >> 31) & jnp.int32(0x7FFFFFFF))
        key_sc[j] = jnp.where(bits == INT_MIN, 0, key)
        return 0

    lax.fori_loop(0, nkb, score_body, 0)

    def count(pred):
        def body(j, acc):
            kb = key_sc[j]
            for c in range(0, tk, LANES):
                acc = acc + jnp.where(pred(kb[:, c:c + LANES]), 1, 0)
            return acc
        acc = lax.fori_loop(0, nkb, body, jnp.zeros((tq, LANES), I32))
        return jnp.sum(acc, axis=1, keepdims=True)

    def bisect(it, ans):
        cand = ans + lax.shift_left(jnp.int32(1), 31 - it)
        ok = count(lambda kb: kb >= cand) >= n_sel
        return jnp.where(ok, cand, ans)

    thr = lax.fori_loop(0, 32, bisect, jnp.full((tq, 1), INT_MIN, I32))
    need = (n_sel - count(lambda kb: kb > thr)).astype(F32)

    upper = (lax.broadcasted_iota(I32, (tk, tk), 0) <= lax.broadcasted_iota(I32, (tk, tk), 1))
    upper = jnp.where(upper, 1.0, 0.0).astype(BF16)

    def sel_body(j, seen):
        kb = key_sc[j]
        tie = jnp.where(kb == thr, 1.0, 0.0)
        rank = seen + jnp.dot(tie.astype(BF16), upper, preferred_element_type=F32)
        take = jnp.where(kb > thr, 1.0, jnp.where(rank <= need, tie, 0.0))
        take = jnp.where(kb == KEY_MASKED, 0.0, take)
        bias_sc[j] = jnp.where(take > 0.5, 0.0, NEG_INF)
        return seen + jnp.sum(tie, axis=1, keepdims=True)

    lax.fori_loop(0, nkb, sel_body, jnp.zeros((tq, 1), F32))

    for p in range(H_B // 2):
        cols = slice(p * LANES, (p + 1) * LANES)
        qp = q_ref[0, :, cols]
        outs = []
        for hh in range(2):
            qh = _head_of_pair(qp, hh)

            def body(j, carry, qh=qh, cols=cols):
                k0 = pl.multiple_of(j * tk, tk)
                s = _qk(qh, k_ref[0, pl.ds(k0, tk), cols]) + bias_sc[j]
                return _online_update(s, v_ref[0, pl.ds(k0, tk), cols], *carry)

            m, l, acc = lax.fori_loop(
                0, nkb, body,
                (jnp.full((tq, 1), NEG_INF, F32), jnp.zeros((tq, 1), F32), jnp.zeros((tq, LANES), F32)))
            outs.append(acc / l)
        o = jnp.where(lane < DH, outs[0], outs[1])
        o_ref[0, :, cols] = (o * _silu(g_ref[0, :, cols].astype(F32))).astype(BF16)


def _dsa_call(zb, sm, k_src, v_src, ik_src, *, b, t, tk_total, tq, tk, past, n_sel, k_col, v_col, ik_col):
    nkb_max = tk_total // tk
    kern = functools.partial(_dsa_kernel, tq=tq, tk=tk, past=past, n_valid=past + t, n_sel=n_sel, nkb_max=nkb_max)
    return pl.pallas_call(
        kern,
        out_shape=jax.ShapeDtypeStruct((b, t, HW), BF16),
        grid=(b, t // tq),
        in_specs=[pl.BlockSpec((1, tq, HW), lambda i, j: (i, j, QB // HW)),
                  pl.BlockSpec((1, tq, HW), lambda i, j: (i, j, IQ // HW)),
                  pl.BlockSpec((1, tq, LANES), lambda i, j: (i, j, 0)),
                  pl.BlockSpec((1, tq, HW), lambda i, j: (i, j, GB // HW)),
                  pl.BlockSpec((1, tk_total, HW), lambda i, j: (i, 0, k_col)),
                  pl.BlockSpec((1, tk_total, HW), lambda i, j: (i, 0, v_col)),
                  pl.BlockSpec((1, tk_total, LANES), lambda i, j: (i, 0, ik_col))],
        out_specs=pl.BlockSpec((1, tq, HW), lambda i, j: (i, j, 0)),
        scratch_shapes=[pltpu.VMEM((nkb_max, tq, tk), I32), pltpu.VMEM((nkb_max, tq, tk), F32)],
        compiler_params=_cparams(2),
        name="dsa",
    )(zb, zb, sm, zb, k_src, v_src, ik_src)


def _mid_kernel(x_ref, ma_ref, mb_ref, wo_ref, mod0_ref, mod1_ref, ng_ref, wc_ref, qg_ref, kvg_ref,
                wuq_ref, wuk_ref, cos_ref, sin_ref,
                x1_ref, g_ref, qlat_ref, qrope_ref, kcat_ref, clat_ref, ckr_ref, *, bt, tt):
    y = (jnp.dot(ma_ref[...], wo_ref[0:HW, :], preferred_element_type=F32)
         + jnp.dot(mb_ref[...], wo_ref[HW:2 * HW, :], preferred_element_type=F32))
    x1 = _gated_residual(x_ref[...], y, mod0_ref, bt, tt)
    x1_ref[...] = x1
    h = _norm_mod(x1, mod1_ref, ng_ref[...], bt, tt).astype(BF16)
    cos = cos_ref[...]
    sin = sin_ref[...]

    def proj(c0, n):
        return jnp.dot(h, wc_ref[:, c0:c0 + n], preferred_element_type=F32)

    g_ref[...] = proj(C_G, H_C * DV).astype(BF16)
    cq = _rms(proj(C_Q, Q_LORA), qg_ref[...]).astype(BF16)
    n_nope = H_C * NOPE
    q_nope = jnp.dot(cq, wuq_ref[:, 0:n_nope], preferred_element_type=F32).astype(BF16)
    for j in range(H_C // 2):
        ql = jnp.dot(q_nope[:, j * LANES:(j + 1) * LANES], wuk_ref[j], preferred_element_type=F32)
        qlat_ref[:, 2 * j * KV_LORA:2 * (j + 1) * KV_LORA] = (ql * C_SCALE).astype(BF16)
    q_rope = jnp.dot(cq, wuq_ref[:, n_nope:], preferred_element_type=F32)
    for c in range(0, H_C * ROPE_DIM, LANES):
        qrope_ref[:, c:c + LANES] = (_rope(q_rope[:, c:c + LANES], cos, sin, ROPE_DIM) * C_SCALE).astype(BF16)
    ckv = _rms(proj(C_KV, KV_LORA), kvg_ref[...])
    clat_ref[...] = ckv
    kr = _rope(proj(C_KR, LANES), cos, sin, ROPE_DIM)
    ckr_ref[...] = kr
    kr = jnp.where(_lane_iota(kr.shape) < ROPE_DIM, kr, 0.0)
    kr4 = kr + pltpu.roll(kr, ROPE_DIM, 1) + pltpu.roll(kr, 2 * ROPE_DIM, 1) + pltpu.roll(kr, 3 * ROPE_DIM, 1)
    kcat_ref[:, 0:KV_LORA] = ckv.astype(BF16)
    kcat_ref[:, KV_LORA:KV_LORA + LANES] = kr4.astype(BF16)


def _mid_call(x2, ma, mb, wo, mod0, mod1, ng, wc, qg, kvg, wuq, wuk_bd, cos, sin, *, t, bt, tt):
    n, d = x2.shape
    tm = bt * tt
    tpb = t // tt
    row = lambda i: (i, 0)
    full2 = lambda i: (0, 0)
    modspec = pl.BlockSpec((bt, 3, d), lambda i: (i * tt // t, 0, 0))
    nq = H_C * KV_LORA
    return pl.pallas_call(
        functools.partial(_mid_kernel, bt=bt, tt=tt),
        out_shape=(jax.ShapeDtypeStruct((n, d), F32),
                   jax.ShapeDtypeStruct((n, H_C * DV), BF16),
                   jax.ShapeDtypeStruct((n, nq), BF16),
                   jax.ShapeDtypeStruct((n, H_C * ROPE_DIM), BF16),
                   jax.ShapeDtypeStruct((n, 2 * LANES), BF16),
                   jax.ShapeDtypeStruct((n, KV_LORA), F32),
                   jax.ShapeDtypeStruct((n, LANES), F32)),
        grid=(n // tm,),
        in_specs=[pl.BlockSpec((tm, d), row),
                  pl.BlockSpec((tm, HW), row), pl.BlockSpec((tm, HW), row),
                  pl.BlockSpec((2 * HW, d), full2),
                  modspec, modspec,
                  pl.BlockSpec((1, d), full2),
                  pl.BlockSpec((d, CW), full2),
                  pl.BlockSpec((1, Q_LORA), full2), pl.BlockSpec((1, KV_LORA), full2),
                  pl.BlockSpec(wuq.shape, full2),
                  pl.BlockSpec(wuk_bd.shape, lambda i: (0, 0, 0)),
                  pl.BlockSpec((tm, LANES), lambda i: (i % tpb, 0)),
                  pl.BlockSpec((tm, LANES), lambda i: (i % tpb, 0))],
        out_specs=(pl.BlockSpec((tm, d), row), pl.BlockSpec((tm, H_C * DV), row),
                   pl.BlockSpec((tm, nq), row), pl.BlockSpec((tm, H_C * ROPE_DIM), row),
                   pl.BlockSpec((tm, 2 * LANES), row), pl.BlockSpec((tm, KV_LORA), row),
                   pl.BlockSpec((tm, LANES), row)),
        compiler_params=_cparams(1),
        name="mid",
    )(x2, ma, mb, wo, mod0, mod1, ng, wc, qg, kvg, wuq, wuk_bd, cos, sin)


def _mla_kernel(qlat_ref, qrope_ref, k_ref, o_ref, qs_sc, *, tq, tk, past, n_valid, nkb_max):
    q0 = past + pl.program_id(1) * tq
    nkb = _num_key_blocks(q0 + tq, tk, nkb_max)
    rows = H_C * tq
    lane = _lane_iota((tq, LANES))
    heads_per_block = LANES // ROPE_DIM
    for h in range(H_C):
        qs_sc[h * tq:(h + 1) * tq, 0:KV_LORA] = qlat_ref[0, :, h * KV_LORA:(h + 1) * KV_LORA]
        blk = qrope_ref[0, :, (h // heads_per_block) * LANES:(h // heads_per_block + 1) * LANES]
        mine = (lane >> 5) == (h % heads_per_block)
        qs_sc[h * tq:(h + 1) * tq, KV_LORA:KV_LORA + LANES] = jnp.where(mine, blk, jnp.zeros_like(blk))
    qchunk = (q0 + lax.broadcasted_iota(I32, (1, tq, 1), 1)) >> CHUNK_SHIFT
    kidx = lax.broadcasted_iota(I32, (1, 1, tk), 2)

    def body(j, carry):
        k0 = pl.multiple_of(j * tk, tk)
        kb = k_ref[0, pl.ds(k0, tk), :]
        s = _qk(qs_sc[...], kb)
        kpos = k0 + kidx
        adm = ((kpos >> CHUNK_SHIFT) <= qchunk) & (kpos < n_valid)
        s = jnp.where(adm, s.reshape(H_C, tq, tk), NEG_INF).reshape(rows, tk)
        return _online_update(s, kb[:, 0:KV_LORA], *carry)

    m, l, acc = lax.fori_loop(
        0, nkb, body,
        (jnp.full((rows, 1), NEG_INF, F32), jnp.zeros((rows, 1), F32), jnp.zeros((rows, KV_LORA), F32)))
    o = (acc / l).astype(BF16)
    for h in range(H_C):
        o_ref[0, :, h * KV_LORA:(h + 1) * KV_LORA] = o[h * tq:(h + 1) * tq]


def _mla_call(qlat, qrope, kcat, *, b, t, tk_total, tq, tk, past):
    nkb_max = tk_total // tk
    nq = H_C * KV_LORA
    return pl.pallas_call(
        functools.partial(_mla_kernel, tq=tq, tk=tk, past=past, n_valid=past + t, nkb_max=nkb_max),
        out_shape=jax.ShapeDtypeStruct((b, t, nq), BF16),
        grid=(b, t // tq),
        in_specs=[pl.BlockSpec((1, tq, nq), lambda i, j: (i, j, 0)),
                  pl.BlockSpec((1, tq, H_C * ROPE_DIM), lambda i, j: (i, j, 0)),
                  pl.BlockSpec((1, tk_total, 2 * LANES), lambda i, j: (i, 0, 0))],
        out_specs=pl.BlockSpec((1, tq, nq), lambda i, j: (i, j, 0)),
        scratch_shapes=[pltpu.VMEM((H_C * tq, 2 * LANES), BF16)],
        compiler_params=_cparams(2),
        name="mla",
    )(qlat, qrope, kcat)


def _out_c_kernel(x1_ref, ol_ref, g_ref, wuv_ref, wo_ref, mod_ref, fg_ref, y_ref, *, bt, tt):
    o = jnp.concatenate(
        [jnp.dot(ol_ref[:, 2 * j * KV_LORA:2 * (j + 1) * KV_LORA], wuv_ref[j], preferred_element_type=F32)
         for j in range(H_C // 2)], axis=1)
    mixed = (o * _silu(g_ref[...].astype(F32))).astype(BF16)
    y = jnp.dot(mixed, wo_ref[...], preferred_element_type=F32)
    x2 = _gated_residual(x1_ref[...], y, mod_ref, bt, tt)
    y_ref[...] = _rms(x2, fg_ref[...])


def _out_c_call(x1, olat, g, wuv_bd, wo, mod1, fg, *, t, bt, tt):
    n, d = x1.shape
    tm = bt * tt
    row = lambda i: (i, 0)
    full2 = lambda i: (0, 0)
    return pl.pallas_call(
        functools.partial(_out_c_kernel, bt=bt, tt=tt),
        out_shape=jax.ShapeDtypeStruct((n, d), F32),
        grid=(n // tm,),
        in_specs=[pl.BlockSpec((tm, d), row),
                  pl.BlockSpec((tm, H_C * KV_LORA), row),
                  pl.BlockSpec((tm, H_C * DV), row),
                  pl.BlockSpec(wuv_bd.shape, lambda i: (0, 0, 0)),
                  pl.BlockSpec(wo.shape, full2),
                  pl.BlockSpec((bt, 3, d), lambda i: (i * tt // t, 0, 0)),
                  pl.BlockSpec((1, d), full2)],
        out_specs=pl.BlockSpec((tm, d), row),
        compiler_params=_cparams(1),
        name="out_c",
    )(x1, olat, g, wuv_bd, wo, mod1, fg)


def _rope_tables(pos, dim, reps_rows):
    half = dim // 2
    inv_freq = ROPE_THETA ** (-jnp.arange(half, dtype=F32) / half)
    ang = pos.astype(F32)[:, None] * inv_freq[None, :]
    cos = jnp.cos(ang)
    sin = jnp.sin(ang)
    cos_d = jnp.concatenate([cos, cos], axis=-1)
    sin_d = jnp.concatenate([-sin, sin], axis=-1)
    reps = LANES // dim
    return jnp.tile(cos_d, (reps_rows, reps)), jnp.tile(sin_d, (reps_rows, reps))


def _prep_weights(ab_w_in, ab_f_bias, ab_w_out, c_w_in, c_w_uq, c_w_uk, c_w_uv, c_w_out):
    d = D_MODEL
    w = ab_w_in
    offs = {}
    start = 0
    for name, size in (("qa", HW), ("ka", HW), ("va", HW), ("fa", H_A), ("ga", HW), ("qb", HW), ("kb", HW),
                       ("vb", HW), ("iq", HW), ("ik", DH), ("iw", H_IDX), ("gb", HW)):
        offs[name] = w[:, start:start + size]
        start += size
    pad = jnp.zeros((d, LANES - DH - H_A - H_IDX), w.dtype)
    w_ab = jnp.concatenate([offs[k] for k in ("qa", "ka", "va", "ga", "qb", "kb", "vb", "iq", "gb",
                                              "ik", "fa", "iw")] + [pad], axis=1).astype(BF16)
    fb = jnp.zeros((1, LANES), F32).at[0, SM_FA:SM_FA + H_A].set(ab_f_bias)

    cq, ckv, kr, g = (c_w_in[:, 0:Q_LORA], c_w_in[:, Q_LORA:Q_LORA + KV_LORA],
                      c_w_in[:, Q_LORA + KV_LORA:Q_LORA + KV_LORA + ROPE_DIM],
                      c_w_in[:, Q_LORA + KV_LORA + ROPE_DIM:])
    w_c = jnp.concatenate([g, cq, ckv, kr, jnp.zeros((d, LANES - ROPE_DIM), c_w_in.dtype)], axis=1).astype(BF16)
    uq = c_w_uq.reshape(Q_LORA, H_C, NOPE + ROPE_DIM)
    w_uq = jnp.concatenate([uq[:, :, :NOPE].reshape(Q_LORA, H_C * NOPE),
                            uq[:, :, NOPE:].reshape(Q_LORA, H_C * ROPE_DIM)], axis=1).astype(BF16)
    uk = jnp.transpose(c_w_uk, (1, 2, 0)).reshape(H_C // 2, 2, NOPE, KV_LORA)
    zk = jnp.zeros_like(uk[:, 0])
    wuk_bd = jnp.concatenate([jnp.concatenate([uk[:, 0], zk], axis=2),
                              jnp.concatenate([zk, uk[:, 1]], axis=2)], axis=1).astype(BF16)
    uv = jnp.transpose(c_w_uv, (1, 0, 2)).reshape(H_C // 2, 2, KV_LORA, DV)
    zv = jnp.zeros_like(uv[:, 0])
    wuv_bd = jnp.concatenate([jnp.concatenate([uv[:, 0], zv], axis=2),
                              jnp.concatenate([zv, uv[:, 1]], axis=2)], axis=1).astype(BF16)
    return dict(w_ab=w_ab, fb=fb, wo_ab=ab_w_out.astype(BF16), w_c=w_c, w_uq=w_uq, wuk_bd=wuk_bd,
                wuv_bd=wuv_bd, wo_c=c_w_out.astype(BF16))


def _pad_keys(x, tk_total):
    return jnp.pad(x, ((0, 0), (0, tk_total - x.shape[1]), (0, 0)))


def _trunk(x, mod0, mod1, caches, wts, norm_g, final_norm_g, c_q_norm_g, c_kv_norm_g, *, bt, tt, tq, tk, tq_c):
    b, t, d = x.shape
    n = b * t
    past = 0 if caches is None else caches[0].shape[1]
    n_keys = past + t
    tk_total = -(-n_keys // tk) * tk
    pos = past + jnp.arange(t)
    cos64, sin64 = _rope_tables(pos, DH, bt)
    cos32, sin32 = _rope_tables(pos, ROPE_DIM, bt)
    x2 = x.reshape(n, d)

    zb, ka, va, kb, vb, sm = _in_ab_call(x2, mod0, norm_g[0:1], wts["w_ab"], cos64, sin64, wts["fb"],
                                         t=t, bt=bt, tt=tt)
    zb3 = zb.reshape(b, t, ZW)
    sm3 = sm.reshape(b, t, LANES)
    ik_new = sm3[:, :, SM_IK:SM_IK + DH]
    logf_new = sm3[:, :, SM_FA:SM_FA + H_A]

    if caches is None:
        logf_all = logf_new
        ka_src = va_src = kb_src = vb_src = ik_src = zb3
        cols = dict(k_a=KA // HW, v_a=VA // HW, k_b=KB // HW, v_b=VB // HW, ik=SM // LANES)
    else:
        c_ak, c_av, c_alogf, c_bk, c_bv, c_bik = caches[:6]

        def cat(cache, new):
            cache = cache.reshape(b, past, -1).astype(BF16)
            return _pad_keys(jnp.concatenate([cache, new], axis=1), tk_total)

        logf_all = jnp.concatenate([c_alogf, logf_new], axis=1)
        ka_src = cat(c_ak, zb3[:, :, KA:KA + HW])
        va_src = cat(c_av, zb3[:, :, VA:VA + HW])
        kb_src = cat(c_bk, zb3[:, :, KB:KB + HW])
        vb_src = cat(c_bv, zb3[:, :, VB:VB + HW])
        ik_src = cat(jnp.concatenate([c_bik, c_bik], axis=-1), zb3[:, :, SM:SM + LANES])
        cols = dict(k_a=0, v_a=0, k_b=0, v_b=0, ik=0)

    logf_t = _pad_keys(logf_all, tk_total).transpose(0, 2, 1)
    cum_t = _cumsum_call(logf_t)
    fq = cum_t[:, :, past:past + t].transpose(0, 2, 1)
    fk = cum_t.reshape(b, H_A, tk_total // tk, tk).transpose(0, 2, 1, 3)

    mix_a = _fox_call(zb3, ka_src, va_src, zb3, fq, fk, b=b, t=t, tk_total=tk_total, tq=tq, tk=tk, past=past,
                      q_col=QA // HW, k_col=cols["k_a"], v_col=cols["v_a"], g_col=GA // HW)
    n_sel = min(TOPK_MAX, n_keys // 4)
    mix_b = _dsa_call(zb3, sm3, kb_src, vb_src, ik_src, b=b, t=t, tk_total=tk_total, tq=tq, tk=tk, past=past,
                      n_sel=n_sel, k_col=cols["k_b"], v_col=cols["v_b"], ik_col=cols["ik"])

    x1, g_c, qlat, qrope, kcat, clat, ckr = _mid_call(
        x2, mix_a.reshape(n, HW), mix_b.reshape(n, HW), wts["wo_ab"], mod0, mod1, norm_g[1:2], wts["w_c"],
        c_q_norm_g, c_kv_norm_g, wts["w_uq"], wts["wuk_bd"], cos32, sin32, t=t, bt=bt, tt=tt)

    kcat3 = kcat.reshape(b, t, 2 * LANES)
    if caches is not None:
        c_lat, c_kr = caches[6:]
        cache_cat = jnp.concatenate([c_lat, c_kr, c_kr, c_kr, c_kr], axis=-1).astype(BF16)
        kcat3 = _pad_keys(jnp.concatenate([cache_cat, kcat3], axis=1), tk_total)
    olat = _mla_call(qlat.reshape(b, t, -1), qrope.reshape(b, t, -1), kcat3,
                     b=b, t=t, tk_total=tk_total, tq=tq_c, tk=tk, past=past)
    y = _out_c_call(x1, olat.reshape(n, -1), g_c, wts["wuv_bd"], wts["wo_c"], mod1, final_norm_g, t=t, bt=bt, tt=tt)

    state = (ka.reshape(1, b, t, H_A, DH), va.reshape(1, b, t, H_A, DH), logf_new[None],
             kb.reshape(1, b, t, H_B, DH), vb.reshape(1, b, t, H_B, DH), ik_new[None],
             clat.reshape(1, b, t, KV_LORA), ckr.reshape(b, t, LANES)[None, :, :, :ROPE_DIM])
    return y.reshape(b, t, d), state


def kernel(x_prompt, x_sample, c_prompt, c_sample, cache_a_k, cache_a_v, cache_a_logf, cache_b_k, cache_b_v, cache_b_idx_k, cache_c_latent, cache_c_krope, norm_g, ada_w, ada_b, final_norm_g, ab_w_in, ab_f_bias, ab_w_out, c_w_in, c_q_norm_g, c_kv_norm_g, c_w_uq, c_w_uk, c_w_uv, c_w_out):
    bp = x_prompt.shape[0]
    bs = x_sample.shape[0]
    d = D_MODEL
    wts = _prep_weights(ab_w_in[0], ab_f_bias[0], ab_w_out[0], c_w_in[0], c_w_uq[0], c_w_uk[0], c_w_uv[0],
                        c_w_out[0])
    mod = _mod_call(jnp.concatenate([c_prompt, c_sample], axis=0), ada_w, ada_b)
    mod = mod.reshape(2, bp + bs, 3, d)
    fg = final_norm_g.reshape(1, d)
    common = dict(wts=wts, norm_g=norm_g, final_norm_g=fg, c_q_norm_g=c_q_norm_g, c_kv_norm_g=c_kv_norm_g)

    y_p, st_p = _trunk(x_prompt, mod[0, :bp], mod[1, :bp], None, bt=1, tt=256, tq=256, tk=256, tq_c=128, **common)
    caches = (cache_a_k[0], cache_a_v[0], cache_a_logf[0], cache_b_k[0], cache_b_v[0], cache_b_idx_k[0],
              cache_c_latent[0], cache_c_krope[0])
    ts = x_sample.shape[1]
    y_s, st_s = _trunk(x_sample, mod[0, bp:], mod[1, bp:], caches, bt=256 // ts, tt=ts, tq=ts, tk=128, tq_c=ts,
                       **common)
    return (y_p, y_s) + st_p + st_s
```

```python
import functools
import struct

import jax
import jax.numpy as jnp
from jax import lax
from jax.experimental import pallas as pl
from jax.experimental.pallas import tpu as pltpu

F32 = jnp.float32
BF16 = jnp.bfloat16
I32 = jnp.int32

D_MODEL = 1024
CHUNK = 64
CHUNK_SHIFT = 6
EPS = 1e-6
NEG_INF = -1e30
ROPE_THETA = 10000.0
H_A = 8
H_B = 8
H_IDX = 8
DH = 64
TOPK_MAX = 256
H_C = 16
Q_LORA = 384
KV_LORA = 128
NOPE = 64
ROPE_DIM = 32
DV = 64
HW = H_A * DH

LANES = 128
VMEM_LIMIT = 56 * 1024 * 1024

QA, KA, VA, GA, QB, KB, VB, IQ, GB, SM = (0, 512, 1024, 1536, 2048, 2560, 3072, 3584, 4096, 4608)
ZW = SM + LANES
SM_IK, SM_FA, SM_IW = 0, 64, 72
C_G, C_Q, C_KV, C_KR = 0, 1024, 1408, 1536
CW = C_KR + LANES

IW_SCALE = H_IDX ** -0.5 * DH ** -0.5
QK_SCALE = DH ** -0.5
C_SCALE = (NOPE + ROPE_DIM) ** -0.5
INT_MIN = -2 ** 31
_NEG_INF_BITS = struct.unpack("<i", struct.pack("<f", NEG_INF))[0]
KEY_MASKED = _NEG_INF_BITS ^ 0x7FFFFFFF
NEW_BLOCK = 128
BISECT_BITS_PER_TRIP = 4
BISECT_CHAIN_ROWS = 64


def _cparams(n_axes):
    return pltpu.CompilerParams(dimension_semantics=("arbitrary",) * n_axes, vmem_limit_bytes=VMEM_LIMIT)


def _sigmoid(x):
    return 1.0 / (1.0 + jnp.exp(-x))


def _silu(x):
    return x * _sigmoid(x)


def _lane_iota(shape):
    return lax.broadcasted_iota(I32, shape, len(shape) - 1)


def _rope(x, cos, sin, dim):
    half = dim // 2
    first = (_lane_iota(x.shape) & (dim - 1)) < half
    rot = jnp.where(first, pltpu.roll(x, LANES - half, 1), pltpu.roll(x, half, 1))
    return x * cos + rot * sin


def _rms(x, g):
    return x * lax.rsqrt(jnp.mean(x * x, axis=-1, keepdims=True) + EPS) * g


def _norm_mod(x, mod_ref, g, bt, tt):
    xn = _rms(x, g)
    shift = mod_ref[:, 0, :]
    scale = mod_ref[:, 1, :]
    if bt == 1:
        return xn * (1.0 + scale) + shift
    xn = xn.reshape(bt, tt, x.shape[-1])
    h = xn * (1.0 + scale)[:, None, :] + shift[:, None, :]
    return h.reshape(bt * tt, x.shape[-1])


def _gated_residual(x, y, mod_ref, bt, tt):
    gate = mod_ref[:, 2, :]
    if bt == 1:
        return x + gate * y
    d = x.shape[-1]
    return (x.reshape(bt, tt, d) + gate[:, None, :] * y.reshape(bt, tt, d)).reshape(bt * tt, d)


def _mod_kernel(c_ref, w_ref, b_ref, o_ref):
    s = _silu(c_ref[...]).astype(BF16)
    o_ref[0] = jnp.dot(s, w_ref[0].astype(BF16), preferred_element_type=F32) + b_ref[0]


def _mod_call(c_all, ada_w, ada_b):
    depth, d, n = ada_w.shape
    rows = c_all.shape[0]
    tn = 512
    return pl.pallas_call(
        _mod_kernel,
        out_shape=jax.ShapeDtypeStruct((depth, rows, n), F32),
        grid=(depth, n // tn),
        in_specs=[pl.BlockSpec((rows, d), lambda l, j: (0, 0)),
                  pl.BlockSpec((1, d, tn), lambda l, j: (l, 0, j)),
                  pl.BlockSpec((1, 1, tn), lambda l, j: (l, 0, j))],
        out_specs=pl.BlockSpec((1, rows, tn), lambda l, j: (l, 0, j)),
        compiler_params=_cparams(2),
        name="mod",
    )(c_all, ada_w, ada_b.reshape(depth, 1, n))


def _in_ab_kernel(x_ref, mod_ref, ng_ref, w_ref, cos_ref, sin_ref, fb_ref,
                  zb_ref, ka_ref, va_ref, kb_ref, vb_ref, sm_ref, *, bt, tt):
    h = _norm_mod(x_ref[...], mod_ref, ng_ref[...], bt, tt).astype(BF16)
    cos = cos_ref[...]
    sin = sin_ref[...]

    def proj(c0, n):
        return jnp.dot(h, w_ref[:, c0:c0 + n], preferred_element_type=F32)

    def rope_cols(z):
        return jnp.concatenate(
            [_rope(z[:, c:c + LANES], cos, sin, DH) for c in range(0, z.shape[1], LANES)], axis=1)

    zb_ref[:, QA:QA + HW] = (proj(QA, HW) * QK_SCALE).astype(BF16)
    z = proj(KA, HW)
    ka_ref[...] = z
    zb_ref[:, KA:KA + HW] = z.astype(BF16)
    z = proj(VA, HW)
    va_ref[...] = z
    zb_ref[:, VA:VA + HW] = z.astype(BF16)
    zb_ref[:, GA:GA + HW] = proj(GA, HW).astype(BF16)
    zb_ref[:, QB:QB + HW] = (rope_cols(proj(QB, HW)) * QK_SCALE).astype(BF16)
    z = rope_cols(proj(KB, HW))
    kb_ref[...] = z
    zb_ref[:, KB:KB + HW] = z.astype(BF16)
    z = proj(VB, HW)
    vb_ref[...] = z
    zb_ref[:, VB:VB + HW] = z.astype(BF16)
    zb_ref[:, IQ:IQ + HW] = rope_cols(proj(IQ, HW)).astype(BF16)
    zb_ref[:, GB:GB + HW] = proj(GB, HW).astype(BF16)

    z = proj(SM, LANES)
    lane = _lane_iota(z.shape)
    ik = _rope(z, cos, sin, DH)
    u = -(z + fb_ref[...])
    logf = -(jnp.maximum(u, 0.0) + jnp.log1p(jnp.exp(-jnp.abs(u))))
    sm_ref[...] = jnp.where(lane < SM_FA, ik, jnp.where(lane < SM_IW, logf, z * IW_SCALE))
    zb_ref[:, SM:SM + LANES] = jnp.where(lane < SM_FA, ik, pltpu.roll(ik, DH, 1)).astype(BF16)


def _in_ab_call(x2, mod, ng, w_ab, cos, sin, fb, *, t, bt, tt):
    n, d = x2.shape
    tm = bt * tt
    tpb = t // tt
    row = lambda i: (i, 0)
    return pl.pallas_call(
        functools.partial(_in_ab_kernel, bt=bt, tt=tt),
        out_shape=(jax.ShapeDtypeStruct((n, ZW), BF16),
                   jax.ShapeDtypeStruct((n, HW), F32), jax.ShapeDtypeStruct((n, HW), F32),
                   jax.ShapeDtypeStruct((n, HW), F32), jax.ShapeDtypeStruct((n, HW), F32),
                   jax.ShapeDtypeStruct((n, LANES), F32)),
        grid=(n // tm,),
        in_specs=[pl.BlockSpec((tm, d), row),
                  pl.BlockSpec((bt, 3, d), lambda i: (i * tt // t, 0, 0)),
                  pl.BlockSpec((1, d), lambda i: (0, 0)),
                  pl.BlockSpec((d, ZW), lambda i: (0, 0)),
                  pl.BlockSpec((tm, LANES), lambda i: (i % tpb, 0)),
                  pl.BlockSpec((tm, LANES), lambda i: (i % tpb, 0)),
                  pl.BlockSpec((1, LANES), lambda i: (0, 0))],
        out_specs=(pl.BlockSpec((tm, ZW), row),
                   pl.BlockSpec((tm, HW), row), pl.BlockSpec((tm, HW), row),
                   pl.BlockSpec((tm, HW), row), pl.BlockSpec((tm, HW), row),
                   pl.BlockSpec((tm, LANES), row)),
        compiler_params=_cparams(1),
        name="in_ab",
    )(x2, mod, ng, w_ab, cos, sin, fb)


def _cumsum_kernel(x_ref, o_ref):
    x = x_ref[0]
    lane = _lane_iota(x.shape)
    s = 1
    while s < x.shape[1]:
        x = x + jnp.where(lane >= s, pltpu.roll(x, s, 1), 0.0)
        s *= 2
    o_ref[0] = x


def _cumsum_call(logf_t):
    b, h, tk = logf_t.shape
    spec = pl.BlockSpec((1, h, tk), lambda i: (i, 0, 0))
    return pl.pallas_call(
        _cumsum_kernel, out_shape=jax.ShapeDtypeStruct(logf_t.shape, F32),
        grid=(b,), in_specs=[spec], out_specs=spec, compiler_params=_cparams(1), name="cumsum",
    )(logf_t)


def _qk(q, k):
    return lax.dot_general(q, k, (((1,), (1,)), ((), ())), preferred_element_type=F32)


def _head_of_pair(x, hh):
    lane = _lane_iota(x.shape)
    keep = (lane < DH) if hh == 0 else (lane >= DH)
    return jnp.where(keep, x, jnp.zeros_like(x))


def _ones_for_other_head(v, hh):
    lane = _lane_iota(v.shape)
    keep = (lane < DH) if hh == 0 else (lane >= DH)
    return jnp.where(keep, v, jnp.ones_like(v))


def _repeat_lanes(x, n):
    return x if n == 1 else jnp.concatenate([x] * n, axis=1)


def _block_max(s):
    cm = s[:, 0:LANES]
    for c in range(LANES, s.shape[1], LANES):
        cm = jnp.maximum(cm, s[:, c:c + LANES])
    return jnp.max(cm, axis=1, keepdims=True)


def _flash_update(idx, s, v_aug, m_sc, acc_sc):
    m_prev = m_sc[idx]
    m_new = jnp.maximum(m_prev, _block_max(s))
    alpha = jnp.exp(m_prev - m_new)
    p = jnp.exp(s - _repeat_lanes(m_new, s.shape[1] // LANES)).astype(BF16)
    acc_sc[idx] = _repeat_lanes(alpha, acc_sc.shape[-1] // LANES) * acc_sc[idx] + jnp.dot(
        p, v_aug, preferred_element_type=F32)
    m_sc[idx] = m_new


def _finish_pairs(acc_of, g_ref, o_ref, n_pairs, rows):
    lane = _lane_iota((rows, LANES))
    for p in range(n_pairs):
        cols = slice(p * LANES, (p + 1) * LANES)
        a0 = acc_of(2 * p)
        a1 = acc_of(2 * p + 1)
        o = jnp.where(lane < DH, a0 / pltpu.roll(a0, DH, 1), a1 / pltpu.roll(a1, DH, 1))
        o_ref[0, :, cols] = (o * _silu(g_ref[0, :, cols].astype(F32))).astype(BF16)


def _single_pass_head(s_c, s_n, vc, vn):
    m = jnp.maximum(jnp.max(s_c, axis=1, keepdims=True), jnp.max(s_n, axis=1, keepdims=True))
    p_c = jnp.exp(s_c - m).astype(BF16)
    p_n = jnp.exp(s_n - m).astype(BF16)
    return (jnp.dot(p_c, vc, preferred_element_type=F32) + jnp.dot(p_n, vn, preferred_element_type=F32))


def _fox_kernel(q_ref, k_ref, v_ref, g_ref, fq_ref, fk_ref, o_ref, qm_sc, fq_sc, m_sc, acc_sc, *, tq):
    tk = tq
    qi = pl.program_id(1)
    for p in range(H_A // 2):
        qp = q_ref[0, :, p * LANES:(p + 1) * LANES]
        for hh in range(2):
            h = 2 * p + hh
            qm_sc[h] = _head_of_pair(qp, hh)
            fq_sc[h] = jnp.broadcast_to(fq_ref[0, :, h:h + 1], (tq, LANES))
    m_sc[...] = jnp.full(m_sc.shape, NEG_INF, F32)
    acc_sc[...] = jnp.zeros(acc_sc.shape, F32)
    n_rep = tk // LANES

    def block(j, diagonal):
        k0 = pl.multiple_of(j * tk, tk)
        if diagonal:
            keep = lax.broadcasted_iota(I32, (tq, tk), 1) <= lax.broadcasted_iota(I32, (tq, tk), 0)
        for p in range(H_A // 2):
            cols = slice(p * LANES, (p + 1) * LANES)
            kp = k_ref[0, pl.ds(k0, tk), cols]
            vp = v_ref[0, pl.ds(k0, tk), cols]
            for hh in range(2):
                h = 2 * p + hh
                s = _qk(qm_sc[h], kp) + (_repeat_lanes(fq_sc[h], n_rep) - fk_ref[0, j, h:h + 1, :])
                if diagonal:
                    s = jnp.where(keep, s, NEG_INF)
                _flash_update(h, s, _ones_for_other_head(vp, hh), m_sc, acc_sc)

    def body(j, carry):
        block(j, False)
        return carry

    lax.fori_loop(0, qi, body, 0)
    block(qi, True)
    _finish_pairs(lambda h: acc_sc[h], g_ref, o_ref, H_A // 2, tq)


def _fox_call(zb3, fq, fk, *, b, t, tq):
    nkb = t // tq
    return pl.pallas_call(
        functools.partial(_fox_kernel, tq=tq),
        out_shape=jax.ShapeDtypeStruct((b, t, HW), BF16),
        grid=(b, t // tq),
        in_specs=[pl.BlockSpec((1, tq, HW), lambda i, j: (i, j, QA // HW)),
                  pl.BlockSpec((1, t, HW), lambda i, j: (i, 0, KA // HW)),
                  pl.BlockSpec((1, t, HW), lambda i, j: (i, 0, VA // HW)),
                  pl.BlockSpec((1, tq, HW), lambda i, j: (i, j, GA // HW)),
                  pl.BlockSpec((1, tq, H_A), lambda i, j: (i, j, 0)),
                  pl.BlockSpec((1, nkb, H_A, tq), lambda i, j: (i, 0, 0, 0))],
        out_specs=pl.BlockSpec((1, tq, HW), lambda i, j: (i, j, 0)),
        scratch_shapes=[pltpu.VMEM((H_A, tq, LANES), BF16), pltpu.VMEM((H_A, tq, LANES), F32),
                        pltpu.VMEM((H_A, tq, LANES), F32), pltpu.VMEM((H_A, tq, LANES), F32)],
        compiler_params=_cparams(2),
        name="fox",
    )(zb3, zb3, zb3, zb3, fq, fk)


def _fox_s_kernel(q_ref, g_ref, kn_ref, vn_ref, kc_ref, vc_ref, fq_ref, fk_ref, o_ref, *, t, past):
    qpos = past + lax.broadcasted_iota(I32, (t, 1), 0)
    keep_n = (past + lax.broadcasted_iota(I32, (1, NEW_BLOCK), 1)) <= qpos
    accs = []
    for p in range(H_A // 2):
        cols = slice(p * LANES, (p + 1) * LANES)
        kc = kc_ref[0, :, cols].astype(BF16)
        vc = vc_ref[0, :, cols].astype(BF16)
        kn = kn_ref[0, :, cols]
        vn = vn_ref[0, :, cols]
        qp = q_ref[0, :, cols]
        for hh in range(2):
            h = 2 * p + hh
            qh = _head_of_pair(qp, hh)
            fq = fq_ref[0, :, h:h + 1]
            s_c = _qk(qh, kc) + fq - fk_ref[0, h:h + 1, 0:past]
            s_n = _qk(qh, kn) + fq - fk_ref[0, h:h + 1, past:past + NEW_BLOCK]
            s_n = jnp.where(keep_n, s_n, NEG_INF)
            accs.append(_single_pass_head(s_c, s_n, _ones_for_other_head(vc, hh), _ones_for_other_head(vn, hh)))
    _finish_pairs(lambda h: accs[h], g_ref, o_ref, H_A // 2, t)


def _fox_s_call(zb3, kn, vn, kc, vc, fq, cum_t, *, b, t, past):
    tk_total = past + NEW_BLOCK
    return pl.pallas_call(
        functools.partial(_fox_s_kernel, t=t, past=past),
        out_shape=jax.ShapeDtypeStruct((b, t, HW), BF16),
        grid=(b,),
        in_specs=[pl.BlockSpec((1, t, HW), lambda i: (i, 0, QA // HW)),
                  pl.BlockSpec((1, t, HW), lambda i: (i, 0, GA // HW)),
                  pl.BlockSpec((1, NEW_BLOCK, HW), lambda i: (i, 0, 0)),
                  pl.BlockSpec((1, NEW_BLOCK, HW), lambda i: (i, 0, 0)),
                  pl.BlockSpec((1, past, HW), lambda i: (i, 0, 0)),
                  pl.BlockSpec((1, past, HW), lambda i: (i, 0, 0)),
                  pl.BlockSpec((1, t, H_A), lambda i: (i, 0, 0)),
                  pl.BlockSpec((1, H_A, tk_total), lambda i: (i, 0, 0))],
        out_specs=pl.BlockSpec((1, t, HW), lambda i: (i, 0, 0)),
        compiler_params=_cparams(1),
        name="fox_s",
    )(zb3, zb3, kn, vn, kc, vc, fq, cum_t)


def _score_keys(iq_of_pair, iw_of_head, ikb, adm, rows):
    acc = jnp.zeros((rows, ikb.shape[0]), F32)
    for p in range(H_IDX // 2):
        iqp = iq_of_pair(p)
        for hh in range(2):
            x = _qk(_head_of_pair(iqp, hh), ikb)
            acc = acc + iw_of_head(2 * p + hh) * jnp.maximum(x, 0.0)
    bits = lax.bitcast_convert_type(jnp.where(adm, acc, NEG_INF), I32)
    key = bits ^ ((bits >> 31) & jnp.int32(0x7FFFFFFF))
    return jnp.where(bits == INT_MIN, 0, key)


def _row_totals(acc):
    ones = jnp.ones((LANES, LANES), BF16)
    return jnp.dot(acc.astype(F32).astype(BF16), ones, preferred_element_type=F32)


def _count_keys(key_sc, n_blocks, r0, rows, pred):
    acc = jnp.zeros((rows, LANES), I32)
    for j in range(n_blocks):
        for c in range(0, key_sc.shape[2], LANES):
            acc = acc + jnp.where(pred(key_sc[j, r0:r0 + rows, c:c + LANES]), 1, 0)
    return _row_totals(acc)


def _topk_threshold(key_sc, n_blocks, n_sel, thr_sc, need_sc):
    total_rows = key_sc.shape[1]
    rows = min(BISECT_CHAIN_ROWS, total_rows)
    chains = total_rows // rows

    def trip(it, anss):
        anss = list(anss)
        for b in range(BISECT_BITS_PER_TRIP):
            inc = lax.shift_left(jnp.int32(1), 31 - (it * BISECT_BITS_PER_TRIP + b))
            for r in range(chains):
                cand = anss[r] + inc
                tot = _count_keys(key_sc, n_blocks, r * rows, rows, lambda kb, cand=cand: kb >= cand)
                anss[r] = jnp.where(tot >= n_sel, cand, anss[r])
        return tuple(anss)

    init = tuple(jnp.full((rows, LANES), INT_MIN, I32) for _ in range(chains))
    anss = lax.fori_loop(0, 32 // BISECT_BITS_PER_TRIP, trip, init)
    for r in range(chains):
        thr = anss[r]
        above = _count_keys(key_sc, n_blocks, r * rows, rows, lambda kb, thr=thr: kb > thr)
        thr_sc[r * rows:(r + 1) * rows, :] = thr
        need_sc[r * rows:(r + 1) * rows, :] = n_sel - above


def _select_block(kb, thr, need, seen):
    w = kb.shape[1]
    n = w // LANES
    thr_w = _repeat_lanes(thr, n)
    upper = lax.broadcasted_iota(I32, (w, w), 0) <= lax.broadcasted_iota(I32, (w, w), 1)
    upper = jnp.where(upper, 1.0, 0.0).astype(BF16)
    tie = jnp.where(kb == thr_w, 1.0, 0.0)
    tie_b = tie.astype(BF16)
    rank = _repeat_lanes(seen, n) + jnp.dot(tie_b, upper, preferred_element_type=F32)
    take = jnp.where(kb > thr_w, 1.0, jnp.where(rank <= _repeat_lanes(need, n), tie, 0.0))
    take = jnp.where(kb == KEY_MASKED, 0.0, take)
    seen = seen + jnp.dot(tie_b, jnp.ones((w, LANES), BF16), preferred_element_type=F32)
    return jnp.where(take > 0.5, 0.0, NEG_INF), seen


def _dsa_kernel(q_ref, iq_ref, sm_ref, g_ref, k_ref, v_ref, ik_ref, o_ref,
                key_sc, bias_sc, thr_sc, need_sc, qm_sc, m_sc, acc_sc, *, tq, n_sel, nkb_max):
    tk = tq
    qi = pl.program_id(1)
    nkb = qi + 1
    qchunk = (qi * tq + lax.broadcasted_iota(I32, (tq, 1), 0)) >> CHUNK_SHIFT
    kidx = lax.broadcasted_iota(I32, (1, tk), 1)

    def score_body(j, carry):
        k0 = pl.multiple_of(j * tk, tk)
        adm = ((k0 + kidx) >> CHUNK_SHIFT) <= qchunk
        key_sc[j] = _score_keys(lambda p: iq_ref[0, :, p * LANES:(p + 1) * LANES],
                                lambda h: sm_ref[0, :, SM_IW + h:SM_IW + h + 1],
                                ik_ref[0, pl.ds(k0, tk), :], adm, tq)
        return carry

    lax.fori_loop(0, nkb, score_body, 0)

    @pl.when(nkb % 2 == 1)
    def _():
        key_sc[jnp.minimum(nkb, nkb_max - 1)] = jnp.full((tq, tk), KEY_MASKED, I32)

    for nv in range(2, nkb_max + 1, 2):
        @pl.when((nkb + 1) // 2 * 2 == nv)
        def _(nv=nv):
            _topk_threshold(key_sc, nv, n_sel, thr_sc, need_sc)

    def sel_body(j, seen):
        bias, seen = _select_block(key_sc[j], thr_sc[...], need_sc[...], seen)
        bias_sc[j] = bias
        return seen

    lax.fori_loop(0, nkb, sel_body, jnp.zeros((tq, LANES), F32))

    for p in range(H_B // 2):
        qp = q_ref[0, :, p * LANES:(p + 1) * LANES]
        for hh in range(2):
            qm_sc[2 * p + hh] = _head_of_pair(qp, hh)
    m_sc[...] = jnp.full(m_sc.shape, NEG_INF, F32)
    acc_sc[...] = jnp.zeros(acc_sc.shape, F32)

    def att_body(j, carry):
        k0 = pl.multiple_of(j * tk, tk)
        bias = bias_sc[j]
        for p in range(H_B // 2):
            cols = slice(p * LANES, (p + 1) * LANES)
            kp = k_ref[0, pl.ds(k0, tk), cols]
            vp = v_ref[0, pl.ds(k0, tk), cols]
            for hh in range(2):
                h = 2 * p + hh
                _flash_update(h, _qk(qm_sc[h], kp) + bias, _ones_for_other_head(vp, hh), m_sc, acc_sc)
        return carry

    lax.fori_loop(0, nkb, att_body, 0)
    _finish_pairs(lambda h: acc_sc[h], g_ref, o_ref, H_B // 2, tq)


def _dsa_call(zb3, sm3, *, b, t, tq, n_sel):
    nkb_max = t // tq
    kern = functools.partial(_dsa_kernel, tq=tq, n_sel=n_sel, nkb_max=nkb_max)
    return pl.pallas_call(
        kern,
        out_shape=jax.ShapeDtypeStruct((b, t, HW), BF16),
        grid=(b, t // tq),
        in_specs=[pl.BlockSpec((1, tq, HW), lambda i, j: (i, j, QB // HW)),
                  pl.BlockSpec((1, tq, HW), lambda i, j: (i, j, IQ // HW)),
                  pl.BlockSpec((1, tq, LANES), lambda i, j: (i, j, 0)),
                  pl.BlockSpec((1, tq, HW), lambda i, j: (i, j, GB // HW)),
                  pl.BlockSpec((1, t, HW), lambda i, j: (i, 0, KB // HW)),
                  pl.BlockSpec((1, t, HW), lambda i, j: (i, 0, VB // HW)),
                  pl.BlockSpec((1, t, LANES), lambda i, j: (i, 0, SM // LANES))],
        out_specs=pl.BlockSpec((1, tq, HW), lambda i, j: (i, j, 0)),
        scratch_shapes=[pltpu.VMEM((nkb_max, tq, tq), I32), pltpu.VMEM((nkb_max, tq, tq), F32),
                        pltpu.VMEM((tq, LANES), I32), pltpu.VMEM((tq, LANES), F32),
                        pltpu.VMEM((H_B, tq, LANES), BF16), pltpu.VMEM((H_B, tq, LANES), F32),
                        pltpu.VMEM((H_B, tq, LANES), F32)],
        compiler_params=_cparams(2),
        name="dsa",
    )(zb3, zb3, sm3, zb3, zb3, zb3, zb3)


def _dsa_sel_s_kernel(iq_ref, sm_ref, ikc_ref, ikn_ref, bias_ref, key_sc, thr_sc, need_sc,
                      *, bt, t, past, n_sel):
    n_blocks = past // LANES + 1
    kpos_n = past + lax.broadcasted_iota(I32, (1, NEW_BLOCK), 1)
    for b in range(bt):
        r0 = b * t
        iq_of_pair = lambda p, b=b: iq_ref[b, :, p * LANES:(p + 1) * LANES]
        iw_of_head = lambda h, b=b: sm_ref[b, :, SM_IW + h:SM_IW + h + 1]
        for j in range(n_blocks - 1):
            key_sc[j, r0:r0 + t, :] = _score_keys(iq_of_pair, iw_of_head,
                                                  ikc_ref[b, j * LANES:(j + 1) * LANES, :], True, t)
        key_sc[n_blocks - 1, r0:r0 + t, :] = _score_keys(iq_of_pair, iw_of_head, ikn_ref[b],
                                                         kpos_n < past + t, t)
    _topk_threshold(key_sc, n_blocks, n_sel, thr_sc, need_sc)
    seen = jnp.zeros((bt * t, LANES), F32)
    for j in range(n_blocks):
        bias, seen = _select_block(key_sc[j], thr_sc[...], need_sc[...], seen)
        for b in range(bt):
            bias_ref[b, :, j * LANES:(j + 1) * LANES] = bias[b * t:(b + 1) * t]


def _dsa_sel_s_call(zb3, sm3, ikc, ikn, *, b, t, past, n_sel, bt):
    tk_total = past + NEW_BLOCK
    n_blocks = tk_total // LANES
    return pl.pallas_call(
        functools.partial(_dsa_sel_s_kernel, bt=bt, t=t, past=past, n_sel=n_sel),
        out_shape=jax.ShapeDtypeStruct((b, t, tk_total), F32),
        grid=(b // bt,),
        in_specs=[pl.BlockSpec((bt, t, HW), lambda i: (i, 0, IQ // HW)),
                  pl.BlockSpec((bt, t, LANES), lambda i: (i, 0, 0)),
                  pl.BlockSpec((bt, past, LANES), lambda i: (i, 0, 0)),
                  pl.BlockSpec((bt, NEW_BLOCK, LANES), lambda i: (i, 0, 0))],
        out_specs=pl.BlockSpec((bt, t, tk_total), lambda i: (i, 0, 0)),
        scratch_shapes=[pltpu.VMEM((n_blocks, bt * t, LANES), I32),
                        pltpu.VMEM((bt * t, LANES), I32), pltpu.VMEM((bt * t, LANES), F32)],
        compiler_params=_cparams(1),
        name="dsa_sel_s",
    )(zb3, sm3, ikc, ikn)


def _dsa_att_s_kernel(q_ref, g_ref, kn_ref, vn_ref, kc_ref, vc_ref, bias_ref, o_ref, *, t, past):
    bias_c = bias_ref[0, :, 0:past]
    bias_n = bias_ref[0, :, past:past + NEW_BLOCK]
    accs = []
    for p in range(H_B // 2):
        cols = slice(p * LANES, (p + 1) * LANES)
        kc = kc_ref[0, :, cols].astype(BF16)
        vc = vc_ref[0, :, cols].astype(BF16)
        kn = kn_ref[0, :, cols]
        vn = vn_ref[0, :, cols]
        qp = q_ref[0, :, cols]
        for hh in range(2):
            qh = _head_of_pair(qp, hh)
            accs.append(_single_pass_head(_qk(qh, kc) + bias_c, _qk(qh, kn) + bias_n,
                                          _ones_for_other_head(vc, hh), _ones_for_other_head(vn, hh)))
    _finish_pairs(lambda h: accs[h], g_ref, o_ref, H_B // 2, t)


def _dsa_att_s_call(zb3, kn, vn, kc, vc, bias, *, b, t, past):
    tk_total = past + NEW_BLOCK
    return pl.pallas_call(
        functools.partial(_dsa_att_s_kernel, t=t, past=past),
        out_shape=jax.ShapeDtypeStruct((b, t, HW), BF16),
        grid=(b,),
        in_specs=[pl.BlockSpec((1, t, HW), lambda i: (i, 0, QB // HW)),
                  pl.BlockSpec((1, t, HW), lambda i: (i, 0, GB // HW)),
                  pl.BlockSpec((1, NEW_BLOCK, HW), lambda i: (i, 0, 0)),
                  pl.BlockSpec((1, NEW_BLOCK, HW), lambda i: (i, 0, 0)),
                  pl.BlockSpec((1, past, HW), lambda i: (i, 0, 0)),
                  pl.BlockSpec((1, past, HW), lambda i: (i, 0, 0)),
                  pl.BlockSpec((1, t, tk_total), lambda i: (i, 0, 0))],
        out_specs=pl.BlockSpec((1, t, HW), lambda i: (i, 0, 0)),
        compiler_params=_cparams(1),
        name="dsa_att_s",
    )(zb3, zb3, kn, vn, kc, vc, bias)


def _mid_kernel(x_ref, ma_ref, mb_ref, wo_ref, mod0_ref, mod1_ref, ng_ref, wc_ref, qg_ref, kvg_ref,
                wuq_ref, wuk_ref, cos_ref, sin_ref,
                x1_ref, g_ref, qlat_ref, qrope_ref, kcat_ref, clat_ref, ckr_ref, *, bt, tt):
    y = (jnp.dot(ma_ref[...], wo_ref[0:HW, :], preferred_element_type=F32)
         + jnp.dot(mb_ref[...], wo_ref[HW:2 * HW, :], preferred_element_type=F32))
    x1 = _gated_residual(x_ref[...], y, mod0_ref, bt, tt)
    x1_ref[...] = x1
    h = _norm_mod(x1, mod1_ref, ng_ref[...], bt, tt).astype(BF16)
    cos = cos_ref[...]
    sin = sin_ref[...]

    def proj(c0, n):
        return jnp.dot(h, wc_ref[:, c0:c0 + n], preferred_element_type=F32)

    g_ref[...] = proj(C_G, H_C * DV).astype(BF16)
    cq = _rms(proj(C_Q, Q_LORA), qg_ref[...]).astype(BF16)
    n_nope = H_C * NOPE
    q_nope = jnp.dot(cq, wuq_ref[:, 0:n_nope], preferred_element_type=F32).astype(BF16)
    for j in range(H_C // 2):
        ql = jnp.dot(q_nope[:, j * LANES:(j + 1) * LANES], wuk_ref[j], preferred_element_type=F32)
        qlat_ref[:, 2 * j * KV_LORA:2 * (j + 1) * KV_LORA] = (ql * C_SCALE).astype(BF16)
    q_rope = jnp.dot(cq, wuq_ref[:, n_nope:], preferred_element_type=F32)
    for c in range(0, H_C * ROPE_DIM, LANES):
        qrope_ref[:, c:c + LANES] = (_rope(q_rope[:, c:c + LANES], cos, sin, ROPE_DIM) * C_SCALE).astype(BF16)
    ckv = _rms(proj(C_KV, KV_LORA), kvg_ref[...])
    clat_ref[...] = ckv
    kr = _rope(proj(C_KR, LANES), cos, sin, ROPE_DIM)
    ckr_ref[...] = kr
    kr = jnp.where(_lane_iota(kr.shape) < ROPE_DIM, kr, 0.0)
    kr4 = kr + pltpu.roll(kr, ROPE_DIM, 1) + pltpu.roll(kr, 2 * ROPE_DIM, 1) + pltpu.roll(kr, 3 * ROPE_DIM, 1)
    kcat_ref[:, 0:KV_LORA] = ckv.astype(BF16)
    kcat_ref[:, KV_LORA:KV_LORA + LANES] = kr4.astype(BF16)


def _mid_call(x2, ma, mb, wo, mod0, mod1, ng, wc, qg, kvg, wuq, wuk_bd, cos, sin, *, t, bt, tt):
    n, d = x2.shape
    tm = bt * tt
    tpb = t // tt
    row = lambda i: (i, 0)
    full2 = lambda i: (0, 0)
    modspec = pl.BlockSpec((bt, 3, d), lambda i: (i * tt // t, 0, 0))
    nq = H_C * KV_LORA
    return pl.pallas_call(
        functools.partial(_mid_kernel, bt=bt, tt=tt),
        out_shape=(jax.ShapeDtypeStruct((n, d), F32),
                   jax.ShapeDtypeStruct((n, H_C * DV), BF16),
                   jax.ShapeDtypeStruct((n, nq), BF16),
                   jax.ShapeDtypeStruct((n, H_C * ROPE_DIM), BF16),
                   jax.ShapeDtypeStruct((n, 2 * LANES), BF16),
                   jax.ShapeDtypeStruct((n, KV_LORA), F32),
                   jax.ShapeDtypeStruct((n, LANES), F32)),
        grid=(n // tm,),
        in_specs=[pl.BlockSpec((tm, d), row),
                  pl.BlockSpec((tm, HW), row), pl.BlockSpec((tm, HW), row),
                  pl.BlockSpec((2 * HW, d), full2),
                  modspec, modspec,
                  pl.BlockSpec((1, d), full2),
                  pl.BlockSpec((d, CW), full2),
                  pl.BlockSpec((1, Q_LORA), full2), pl.BlockSpec((1, KV_LORA), full2),
                  pl.BlockSpec(wuq.shape, full2),
                  pl.BlockSpec(wuk_bd.shape, lambda i: (0, 0, 0)),
                  pl.BlockSpec((tm, LANES), lambda i: (i % tpb, 0)),
                  pl.BlockSpec((tm, LANES), lambda i: (i % tpb, 0))],
        out_specs=(pl.BlockSpec((tm, d), row), pl.BlockSpec((tm, H_C * DV), row),
                   pl.BlockSpec((tm, nq), row), pl.BlockSpec((tm, H_C * ROPE_DIM), row),
                   pl.BlockSpec((tm, 2 * LANES), row), pl.BlockSpec((tm, KV_LORA), row),
                   pl.BlockSpec((tm, LANES), row)),
        compiler_params=_cparams(1),
        name="mid",
    )(x2, ma, mb, wo, mod0, mod1, ng, wc, qg, kvg, wuq, wuk_bd, cos, sin)


def _stack_heads(qlat_ref, qrope_ref, qs_sc, b, tq):
    lane = _lane_iota((tq, LANES))
    heads_per_block = LANES // ROPE_DIM
    for h in range(H_C):
        qs_sc[h * tq:(h + 1) * tq, 0:KV_LORA] = qlat_ref[b, :, h * KV_LORA:(h + 1) * KV_LORA]
        blk = qrope_ref[b, :, (h // heads_per_block) * LANES:(h // heads_per_block + 1) * LANES]
        mine = (lane >> 5) == (h % heads_per_block)
        qs_sc[h * tq:(h + 1) * tq, KV_LORA:KV_LORA + LANES] = jnp.where(mine, blk, jnp.zeros_like(blk))


def _latent_with_ones(lat):
    return jnp.concatenate([lat, jnp.ones_like(lat)], axis=1)


def _mla_kernel(qlat_ref, qrope_ref, k_ref, o_ref, qs_sc, m_sc, acc_sc, *, tq, tk):
    qi = pl.program_id(1)
    rows = H_C * tq
    _stack_heads(qlat_ref, qrope_ref, qs_sc, 0, tq)
    m_sc[...] = jnp.full(m_sc.shape, NEG_INF, F32)
    acc_sc[...] = jnp.zeros(acc_sc.shape, F32)
    q_end = (qi + 1) * tq
    n_full = (qi * tq) // tk
    nkb = (q_end + tk - 1) // tk
    qchunk = (qi * tq + lax.broadcasted_iota(I32, (1, tq, 1), 1)) >> CHUNK_SHIFT
    kidx = lax.broadcasted_iota(I32, (1, 1, tk), 2)

    def block(j, masked):
        k0 = pl.multiple_of(j * tk, tk)
        kb = k_ref[0, pl.ds(k0, tk), :]
        s = _qk(qs_sc[...], kb)
        if masked:
            adm = ((k0 + kidx) >> CHUNK_SHIFT) <= qchunk
            s = jnp.where(adm, s.reshape(H_C, tq, tk), NEG_INF).reshape(rows, tk)
        _flash_update(0, s, _latent_with_ones(kb[:, 0:KV_LORA]), m_sc, acc_sc)

    def body_full(j, carry):
        block(j, False)
        return carry

    def body_masked(j, carry):
        block(j, True)
        return carry

    lax.fori_loop(0, n_full, body_full, 0)
    lax.fori_loop(n_full, nkb, body_masked, 0)
    acc = acc_sc[0]
    o = (acc[:, 0:KV_LORA] / acc[:, KV_LORA:2 * KV_LORA]).astype(BF16)
    for h in range(H_C):
        o_ref[0, :, h * KV_LORA:(h + 1) * KV_LORA] = o[h * tq:(h + 1) * tq]


def _mla_call(qlat, qrope, kcat, *, b, t, tq, tk):
    nq = H_C * KV_LORA
    rows = H_C * tq
    return pl.pallas_call(
        functools.partial(_mla_kernel, tq=tq, tk=tk),
        out_shape=jax.ShapeDtypeStruct((b, t, nq), BF16),
        grid=(b, t // tq),
        in_specs=[pl.BlockSpec((1, tq, nq), lambda i, j: (i, j, 0)),
                  pl.BlockSpec((1, tq, H_C * ROPE_DIM), lambda i, j: (i, j, 0)),
                  pl.BlockSpec((1, t, 2 * LANES), lambda i, j: (i, 0, 0))],
        out_specs=pl.BlockSpec((1, tq, nq), lambda i, j: (i, j, 0)),
        scratch_shapes=[pltpu.VMEM((rows, 2 * LANES), BF16), pltpu.VMEM((1, rows, LANES), F32),
                        pltpu.VMEM((1, rows, 2 * LANES), F32)],
        compiler_params=_cparams(2),
        name="mla",
    )(qlat, qrope, kcat)


def _mla_s_kernel(qlat_ref, qrope_ref, latc_ref, krc_ref, kn_ref, o_ref, qs_sc, *, bt, t, past):
    keep_n = (past + lax.broadcasted_iota(I32, (1, NEW_BLOCK), 1)) < past + t
    for b in range(bt):
        _stack_heads(qlat_ref, qrope_ref, qs_sc.at[b], b, t)
        q = qs_sc[b]
        lat = latc_ref[b].astype(BF16)
        kc = jnp.concatenate([lat, krc_ref[b]], axis=1)
        kn = kn_ref[b]
        s_c = _qk(q, kc)
        s_n = jnp.where(keep_n, _qk(q, kn), NEG_INF)
        acc = _single_pass_head(s_c, s_n, _latent_with_ones(lat), _latent_with_ones(kn[:, 0:KV_LORA]))
        o = (acc[:, 0:KV_LORA] / acc[:, KV_LORA:2 * KV_LORA]).astype(BF16)
        for h in range(H_C):
            o_ref[b, :, h * KV_LORA:(h + 1) * KV_LORA] = o[h * t:(h + 1) * t]


def _mla_s_call(qlat, qrope, latc, krc, kn, *, b, t, past, bt):
    nq = H_C * KV_LORA
    return pl.pallas_call(
        functools.partial(_mla_s_kernel, bt=bt, t=t, past=past),
        out_shape=jax.ShapeDtypeStruct((b, t, nq), BF16),
        grid=(b // bt,),
        in_specs=[pl.BlockSpec((bt, t, nq), lambda i: (i, 0, 0)),
                  pl.BlockSpec((bt, t, H_C * ROPE_DIM), lambda i: (i, 0, 0)),
                  pl.BlockSpec((bt, past, KV_LORA), lambda i: (i, 0, 0)),
                  pl.BlockSpec((bt, past, LANES), lambda i: (i, 0, 0)),
                  pl.BlockSpec((bt, NEW_BLOCK, 2 * LANES), lambda i: (i, 0, 0))],
        out_specs=pl.BlockSpec((bt, t, nq), lambda i: (i, 0, 0)),
        scratch_shapes=[pltpu.VMEM((bt, H_C * t, 2 * LANES), BF16)],
        compiler_params=_cparams(1),
        name="mla_s",
    )(qlat, qrope, latc, krc, kn)


def _out_c_kernel(x1_ref, ol_ref, g_ref, wuv_ref, wo_ref, mod_ref, fg_ref, y_ref, *, bt, tt):
    o = jnp.concatenate(
        [jnp.dot(ol_ref[:, 2 * j * KV_LORA:2 * (j + 1) * KV_LORA], wuv_ref[j], preferred_element_type=F32)
         for j in range(H_C // 2)], axis=1)
    mixed = (o * _silu(g_ref[...].astype(F32))).astype(BF16)
    y = jnp.dot(mixed, wo_ref[...], preferred_element_type=F32)
    x2 = _gated_residual(x1_ref[...], y, mod_ref, bt, tt)
    y_ref[...] = _rms(x2, fg_ref[...])


def _out_c_call(x1, olat, g, wuv_bd, wo, mod1, fg, *, t, bt, tt):
    n, d = x1.shape
    tm = bt * tt
    row = lambda i: (i, 0)
    full2 = lambda i: (0, 0)
    return pl.pallas_call(
        functools.partial(_out_c_kernel, bt=bt, tt=tt),
        out_shape=jax.ShapeDtypeStruct((n, d), F32),
        grid=(n // tm,),
        in_specs=[pl.BlockSpec((tm, d), row),
                  pl.BlockSpec((tm, H_C * KV_LORA), row),
                  pl.BlockSpec((tm, H_C * DV), row),
                  pl.BlockSpec(wuv_bd.shape, lambda i: (0, 0, 0)),
                  pl.BlockSpec(wo.shape, full2),
                  pl.BlockSpec((bt, 3, d), lambda i: (i * tt // t, 0, 0)),
                  pl.BlockSpec((1, d), full2)],
        out_specs=pl.BlockSpec((tm, d), row),
        compiler_params=_cparams(1),
        name="out_c",
    )(x1, olat, g, wuv_bd, wo, mod1, fg)


def _rope_tables(pos, dim, reps_rows):
    half = dim // 2
    inv_freq = ROPE_THETA ** (-jnp.arange(half, dtype=F32) / half)
    ang = pos.astype(F32)[:, None] * inv_freq[None, :]
    cos = jnp.cos(ang)
    sin = jnp.sin(ang)
    cos_d = jnp.concatenate([cos, cos], axis=-1)
    sin_d = jnp.concatenate([-sin, sin], axis=-1)
    reps = LANES // dim
    return jnp.tile(cos_d, (reps_rows, reps)), jnp.tile(sin_d, (reps_rows, reps))


def _prep_weights(ab_w_in, ab_f_bias, ab_w_out, c_w_in, c_w_uq, c_w_uk, c_w_uv, c_w_out):
    d = D_MODEL
    w = ab_w_in
    offs = {}
    start = 0
    for name, size in (("qa", HW), ("ka", HW), ("va", HW), ("fa", H_A), ("ga", HW), ("qb", HW), ("kb", HW),
                       ("vb", HW), ("iq", HW), ("ik", DH), ("iw", H_IDX), ("gb", HW)):
        offs[name] = w[:, start:start + size]
        start += size
    pad = jnp.zeros((d, LANES - DH - H_A - H_IDX), w.dtype)
    w_ab = jnp.concatenate([offs[k] for k in ("qa", "ka", "va", "ga", "qb", "kb", "vb", "iq", "gb",
                                              "ik", "fa", "iw")] + [pad], axis=1).astype(BF16)
    fb = jnp.zeros((1, LANES), F32).at[0, SM_FA:SM_FA + H_A].set(ab_f_bias)

    cq, ckv, kr, g = (c_w_in[:, 0:Q_LORA], c_w_in[:, Q_LORA:Q_LORA + KV_LORA],
                      c_w_in[:, Q_LORA + KV_LORA:Q_LORA + KV_LORA + ROPE_DIM],
                      c_w_in[:, Q_LORA + KV_LORA + ROPE_DIM:])
    w_c = jnp.concatenate([g, cq, ckv, kr, jnp.zeros((d, LANES - ROPE_DIM), c_w_in.dtype)], axis=1).astype(BF16)
    uq = c_w_uq.reshape(Q_LORA, H_C, NOPE + ROPE_DIM)
    w_uq = jnp.concatenate([uq[:, :, :NOPE].reshape(Q_LORA, H_C * NOPE),
                            uq[:, :, NOPE:].reshape(Q_LORA, H_C * ROPE_DIM)], axis=1).astype(BF16)
    uk = jnp.transpose(c_w_uk, (1, 2, 0)).reshape(H_C // 2, 2, NOPE, KV_LORA)
    zk = jnp.zeros_like(uk[:, 0])
    wuk_bd = jnp.concatenate([jnp.concatenate([uk[:, 0], zk], axis=2),
                              jnp.concatenate([zk, uk[:, 1]], axis=2)], axis=1).astype(BF16)
    uv = jnp.transpose(c_w_uv, (1, 0, 2)).reshape(H_C // 2, 2, KV_LORA, DV)
    zv = jnp.zeros_like(uv[:, 0])
    wuv_bd = jnp.concatenate([jnp.concatenate([uv[:, 0], zv], axis=2),
                              jnp.concatenate([zv, uv[:, 1]], axis=2)], axis=1).astype(BF16)
    return dict(w_ab=w_ab, fb=fb, wo_ab=ab_w_out.astype(BF16), w_c=w_c, w_uq=w_uq, wuk_bd=wuk_bd,
                wuv_bd=wuv_bd, wo_c=c_w_out.astype(BF16))


def _pad_rows(x, rows):
    return jnp.pad(x, ((0, 0), (0, rows - x.shape[1]), (0, 0)))


def _trunk(x, mod0, mod1, caches, wts, norm_g, final_norm_g, c_q_norm_g, c_kv_norm_g, *, bt, tt, tq, tq_c, tk_c):
    b, t, d = x.shape
    n = b * t
    past = 0 if caches is None else caches[0].shape[1]
    n_keys = past + t
    pos = past + jnp.arange(t)
    cos64, sin64 = _rope_tables(pos, DH, bt)
    cos32, sin32 = _rope_tables(pos, ROPE_DIM, bt)
    x2 = x.reshape(n, d)
    n_sel = min(TOPK_MAX, n_keys // 4)

    zb, ka, va, kb, vb, sm = _in_ab_call(x2, mod0, norm_g[0:1], wts["w_ab"], cos64, sin64, wts["fb"],
                                         t=t, bt=bt, tt=tt)
    zb3 = zb.reshape(b, t, ZW)
    sm3 = sm.reshape(b, t, LANES)
    ik_new = sm3[:, :, SM_IK:SM_IK + DH]
    logf_new = sm3[:, :, SM_FA:SM_FA + H_A]

    if caches is None:
        cum_t = _cumsum_call(logf_new.transpose(0, 2, 1))
        fq = cum_t.transpose(0, 2, 1)
        fk = cum_t.reshape(b, H_A, t // tq, tq).transpose(0, 2, 1, 3)
        mix_a = _fox_call(zb3, fq, fk, b=b, t=t, tq=tq)
        mix_b = _dsa_call(zb3, sm3, b=b, t=t, tq=tq, n_sel=n_sel)
    else:
        c_ak, c_av, c_alogf, c_bk, c_bv, c_bik = caches[:6]
        tk_total = past + NEW_BLOCK
        logf_all = _pad_rows(jnp.concatenate([c_alogf, logf_new], axis=1), tk_total)
        cum_t = _cumsum_call(logf_all.transpose(0, 2, 1))
        fq = cum_t[:, :, past:past + t].transpose(0, 2, 1)
        new = lambda c0, w: _pad_rows(zb3[:, :, c0:c0 + w], NEW_BLOCK)
        mix_a = _fox_s_call(zb3, new(KA, HW), new(VA, HW), c_ak.reshape(b, past, HW), c_av.reshape(b, past, HW),
                            fq, cum_t, b=b, t=t, past=past)
        ikc = jnp.concatenate([c_bik, c_bik], axis=-1).astype(BF16)
        bias = _dsa_sel_s_call(zb3, sm3, ikc, new(SM, LANES), b=b, t=t, past=past, n_sel=n_sel, bt=8)
        mix_b = _dsa_att_s_call(zb3, new(KB, HW), new(VB, HW), c_bk.reshape(b, past, HW),
                                c_bv.reshape(b, past, HW), bias, b=b, t=t, past=past)

    x1, g_c, qlat, qrope, kcat, clat, ckr = _mid_call(
        x2, mix_a.reshape(n, HW), mix_b.reshape(n, HW), wts["wo_ab"], mod0, mod1, norm_g[1:2], wts["w_c"],
        c_q_norm_g, c_kv_norm_g, wts["w_uq"], wts["wuk_bd"], cos32, sin32, t=t, bt=bt, tt=tt)

    kcat3 = kcat.reshape(b, t, 2 * LANES)
    qlat3 = qlat.reshape(b, t, -1)
    qrope3 = qrope.reshape(b, t, -1)
    if caches is None:
        olat = _mla_call(qlat3, qrope3, kcat3, b=b, t=t, tq=tq_c, tk=tk_c)
    else:
        c_lat, c_kr = caches[6:]
        krc = jnp.concatenate([c_kr] * (LANES // ROPE_DIM), axis=-1).astype(BF16)
        olat = _mla_s_call(qlat3, qrope3, c_lat, krc, _pad_rows(kcat3, NEW_BLOCK), b=b, t=t, past=past, bt=2)
    y = _out_c_call(x1, olat.reshape(n, -1), g_c, wts["wuv_bd"], wts["wo_c"], mod1, final_norm_g, t=t, bt=bt, tt=tt)

    state = (ka.reshape(1, b, t, H_A, DH), va.reshape(1, b, t, H_A, DH), logf_new[None],
             kb.reshape(1, b, t, H_B, DH), vb.reshape(1, b, t, H_B, DH), ik_new[None],
             clat.reshape(1, b, t, KV_LORA), ckr.reshape(b, t, LANES)[None, :, :, :ROPE_DIM])
    return y.reshape(b, t, d), state


def kernel(x_prompt, x_sample, c_prompt, c_sample, cache_a_k, cache_a_v, cache_a_logf, cache_b_k, cache_b_v, cache_b_idx_k, cache_c_latent, cache_c_krope, norm_g, ada_w, ada_b, final_norm_g, ab_w_in, ab_f_bias, ab_w_out, c_w_in, c_q_norm_g, c_kv_norm_g, c_w_uq, c_w_uk, c_w_uv, c_w_out):
    bp = x_prompt.shape[0]
    d = D_MODEL
    wts = _prep_weights(ab_w_in[0], ab_f_bias[0], ab_w_out[0], c_w_in[0], c_w_uq[0], c_w_uk[0], c_w_uv[0],
                        c_w_out[0])
    mod = _mod_call(jnp.concatenate([c_prompt, c_sample], axis=0), ada_w, ada_b)
    mod = mod.reshape(2, -1, 3, d)
    fg = final_norm_g.reshape(1, d)
    common = dict(wts=wts, norm_g=norm_g, final_norm_g=fg, c_q_norm_g=c_q_norm_g, c_kv_norm_g=c_kv_norm_g)

    y_p, st_p = _trunk(x_prompt, mod[0, :bp], mod[1, :bp], None, bt=1, tt=256, tq=256, tq_c=128, tk_c=256,
                       **common)
    caches = (cache_a_k[0], cache_a_v[0], cache_a_logf[0], cache_b_k[0], cache_b_v[0], cache_b_idx_k[0],
              cache_c_latent[0], cache_c_krope[0])
    ts = x_sample.shape[1]
    y_s, st_s = _trunk(x_sample, mod[0, bp:], mod[1, bp:], caches, bt=256 // ts, tt=ts, tq=ts, tq_c=ts, tk_c=0,
                       **common)
    return (y_p, y_s) + st_p + st_s
```

```python
import functools
import struct

import jax
import jax.numpy as jnp
from jax import lax
from jax.experimental import pallas as pl
from jax.experimental.pallas import tpu as pltpu

F32 = jnp.float32
BF16 = jnp.bfloat16
I32 = jnp.int32
I16 = jnp.int16

D_MODEL = 1024
CHUNK = 64
CHUNK_SHIFT = 6
EPS = 1e-6
NEG_INF = -1e30
ROPE_THETA = 10000.0
H_A = 8
H_B = 8
H_IDX = 8
DH = 64
TOPK_MAX = 256
H_C = 16
Q_LORA = 384
KV_LORA = 128
NOPE = 64
ROPE_DIM = 32
DV = 64
HW = H_A * DH

LANES = 128
VMEM_LIMIT = 56 * 1024 * 1024

QA, KA, VA, GA, QB, KB, VB, IQ, GB, SM = (0, 512, 1024, 1536, 2048, 2560, 3072, 3584, 4096, 4608)
ZW = SM + LANES
SM_IK, SM_FA, SM_IW = 0, 64, 72
C_G, C_Q, C_KV, C_KR = 0, 1024, 1408, 1536
CW = C_KR + LANES

IW_SCALE = H_IDX ** -0.5 * DH ** -0.5
LOG2E = 1.4426950408889634
QK_SCALE = DH ** -0.5 * LOG2E
C_SCALE = (NOPE + ROPE_DIM) ** -0.5 * LOG2E
INT_MIN = -2 ** 31
_NEG_INF_BITS = struct.unpack("<i", struct.pack("<f", NEG_INF))[0]
KEY_MASKED = _NEG_INF_BITS ^ 0x7FFFFFFF
NEW_BLOCK = 128
BISECT_BITS_PER_TRIP = 4
BISECT_TWO_BIT_MAX_KEYS = 512
BISECT_CHAIN_ROWS = 64
I16_MIN = -2 ** 15
IDX_HEAD_GROUPS = 2
SELECT_WIDTH = 256


def _cparams(n_axes):
    return pltpu.CompilerParams(dimension_semantics=("arbitrary",) * n_axes, vmem_limit_bytes=VMEM_LIMIT)


def _sigmoid(x):
    return 1.0 / (1.0 + jnp.exp(-x))


def _silu(x):
    return x * _sigmoid(x)


def _lane_iota(shape):
    return lax.broadcasted_iota(I32, shape, len(shape) - 1)


def _rope(x, cos, sin, dim):
    half = dim // 2
    first = (_lane_iota(x.shape) & (dim - 1)) < half
    rot = jnp.where(first, pltpu.roll(x, LANES - half, 1), pltpu.roll(x, half, 1))
    return x * cos + rot * sin


def _rms(x, g):
    return x * lax.rsqrt(jnp.mean(x * x, axis=-1, keepdims=True) + EPS) * g


def _norm_mod(x, mod_ref, g, bt, tt):
    xn = _rms(x, g)
    shift = mod_ref[:, 0, :]
    scale = mod_ref[:, 1, :]
    if bt == 1:
        return xn * (1.0 + scale) + shift
    xn = xn.reshape(bt, tt, x.shape[-1])
    h = xn * (1.0 + scale)[:, None, :] + shift[:, None, :]
    return h.reshape(bt * tt, x.shape[-1])


def _gated_residual(x, y, mod_ref, bt, tt):
    gate = mod_ref[:, 2, :]
    if bt == 1:
        return x + gate * y
    d = x.shape[-1]
    return (x.reshape(bt, tt, d) + gate[:, None, :] * y.reshape(bt, tt, d)).reshape(bt * tt, d)


def _mod_kernel(c_ref, w_ref, b_ref, o_ref):
    s = _silu(c_ref[...]).astype(BF16)
    o_ref[0] = jnp.dot(s, w_ref[0].astype(BF16), preferred_element_type=F32) + b_ref[0]


def _mod_call(c_all, ada_w, ada_b):
    depth, d, n = ada_w.shape
    rows = c_all.shape[0]
    tn = 512
    return pl.pallas_call(
        _mod_kernel,
        out_shape=jax.ShapeDtypeStruct((depth, rows, n), F32),
        grid=(depth, n // tn),
        in_specs=[pl.BlockSpec((rows, d), lambda l, j: (0, 0)),
                  pl.BlockSpec((1, d, tn), lambda l, j: (l, 0, j)),
                  pl.BlockSpec((1, 1, tn), lambda l, j: (l, 0, j))],
        out_specs=pl.BlockSpec((1, rows, tn), lambda l, j: (l, 0, j)),
        compiler_params=_cparams(2),
        name="mod",
    )(c_all, ada_w, ada_b.reshape(depth, 1, n))


def _in_ab_kernel(x_ref, mod_ref, ng_ref, w_ref, cos_ref, sin_ref, fb_ref,
                  zb_ref, ka_ref, va_ref, kb_ref, vb_ref, sm_ref, *, bt, tt):
    h = _norm_mod(x_ref[...], mod_ref, ng_ref[...], bt, tt).astype(BF16)
    cos = cos_ref[...]
    sin = sin_ref[...]

    def proj(c0, n):
        return jnp.dot(h, w_ref[:, c0:c0 + n], preferred_element_type=F32)

    def rope_cols(z):
        return jnp.concatenate(
            [_rope(z[:, c:c + LANES], cos, sin, DH) for c in range(0, z.shape[1], LANES)], axis=1)

    zb_ref[:, QA:QA + HW] = (proj(QA, HW) * QK_SCALE).astype(BF16)
    z = proj(KA, HW)
    ka_ref[...] = z
    zb_ref[:, KA:KA + HW] = z.astype(BF16)
    z = proj(VA, HW)
    va_ref[...] = z
    zb_ref[:, VA:VA + HW] = z.astype(BF16)
    zb_ref[:, GA:GA + HW] = proj(GA, HW).astype(BF16)
    zb_ref[:, QB:QB + HW] = (rope_cols(proj(QB, HW)) * QK_SCALE).astype(BF16)
    z = rope_cols(proj(KB, HW))
    kb_ref[...] = z
    zb_ref[:, KB:KB + HW] = z.astype(BF16)
    z = proj(VB, HW)
    vb_ref[...] = z
    zb_ref[:, VB:VB + HW] = z.astype(BF16)
    zb_ref[:, IQ:IQ + HW] = rope_cols(proj(IQ, HW)).astype(BF16)
    zb_ref[:, GB:GB + HW] = proj(GB, HW).astype(BF16)

    z = proj(SM, LANES)
    lane = _lane_iota(z.shape)
    ik = _rope(z, cos, sin, DH)
    u = -(z + fb_ref[...])
    logf = -(jnp.maximum(u, 0.0) + jnp.log1p(jnp.exp(-jnp.abs(u))))
    sm_ref[...] = jnp.where(lane < SM_FA, ik, jnp.where(lane < SM_IW, logf, z * IW_SCALE))
    zb_ref[:, SM:SM + LANES] = jnp.where(lane < SM_FA, ik, pltpu.roll(ik, DH, 1)).astype(BF16)


def _in_ab_call(x2, mod, ng, w_ab, cos, sin, fb, *, t, bt, tt):
    n, d = x2.shape
    tm = bt * tt
    tpb = t // tt
    row = lambda i: (i, 0)
    return pl.pallas_call(
        functools.partial(_in_ab_kernel, bt=bt, tt=tt),
        out_shape=(jax.ShapeDtypeStruct((n, ZW), BF16),
                   jax.ShapeDtypeStruct((n, HW), F32), jax.ShapeDtypeStruct((n, HW), F32),
                   jax.ShapeDtypeStruct((n, HW), F32), jax.ShapeDtypeStruct((n, HW), F32),
                   jax.ShapeDtypeStruct((n, LANES), F32)),
        grid=(n // tm,),
        in_specs=[pl.BlockSpec((tm, d), row),
                  pl.BlockSpec((bt, 3, d), lambda i: (i * tt // t, 0, 0)),
                  pl.BlockSpec((1, d), lambda i: (0, 0)),
                  pl.BlockSpec((d, ZW), lambda i: (0, 0)),
                  pl.BlockSpec((tm, LANES), lambda i: (i % tpb, 0)),
                  pl.BlockSpec((tm, LANES), lambda i: (i % tpb, 0)),
                  pl.BlockSpec((1, LANES), lambda i: (0, 0))],
        out_specs=(pl.BlockSpec((tm, ZW), row),
                   pl.BlockSpec((tm, HW), row), pl.BlockSpec((tm, HW), row),
                   pl.BlockSpec((tm, HW), row), pl.BlockSpec((tm, HW), row),
                   pl.BlockSpec((tm, LANES), row)),
        compiler_params=_cparams(1),
        name="in_ab",
    )(x2, mod, ng, w_ab, cos, sin, fb)


def _cumsum_kernel(x_ref, o_ref):
    x = x_ref[0]
    lane = _lane_iota(x.shape)
    s = 1
    while s < x.shape[1]:
        x = x + jnp.where(lane >= s, pltpu.roll(x, s, 1), 0.0)
        s *= 2
    o_ref[0] = x * LOG2E


def _cumsum_call(logf_t):
    b, h, tk = logf_t.shape
    spec = pl.BlockSpec((1, h, tk), lambda i: (i, 0, 0))
    return pl.pallas_call(
        _cumsum_kernel, out_shape=jax.ShapeDtypeStruct(logf_t.shape, F32),
        grid=(b,), in_specs=[spec], out_specs=spec, compiler_params=_cparams(1), name="cumsum",
    )(logf_t)


def _qk(q, k):
    return lax.dot_general(q, k, (((1,), (1,)), ((), ())), preferred_element_type=F32)


def _head_of_pair(x, hh):
    lane = _lane_iota(x.shape)
    keep = (lane < DH) if hh == 0 else (lane >= DH)
    return jnp.where(keep, x, jnp.zeros_like(x))


def _ones_for_other_head(v, hh):
    lane = _lane_iota(v.shape)
    keep = (lane < DH) if hh == 0 else (lane >= DH)
    return jnp.where(keep, v, jnp.ones_like(v))


def _repeat_lanes(x, n):
    return x if n == 1 else jnp.concatenate([x] * n, axis=1)


def _block_max(s):
    cm = s[:, 0:LANES]
    for c in range(LANES, s.shape[1], LANES):
        cm = jnp.maximum(cm, s[:, c:c + LANES])
    return jnp.max(cm, axis=1, keepdims=True)


def _flash_update(idx, s, v_aug, m_sc, acc_sc):
    m_prev = m_sc[idx]
    m_new = jnp.maximum(m_prev, _block_max(s))
    alpha = jnp.exp2(m_prev - m_new)
    p = jnp.exp2(s - _repeat_lanes(m_new, s.shape[1] // LANES)).astype(BF16)
    acc_sc[idx] = _repeat_lanes(alpha, acc_sc.shape[-1] // LANES) * acc_sc[idx] + jnp.dot(
        p, v_aug, preferred_element_type=F32)
    m_sc[idx] = m_new


def _finish_pairs(acc_of, g_ref, o_ref, n_pairs, rows):
    lane = _lane_iota((rows, LANES))
    for p in range(n_pairs):
        cols = slice(p * LANES, (p + 1) * LANES)
        a0 = acc_of(2 * p)
        a1 = acc_of(2 * p + 1)
        o = jnp.where(lane < DH, a0 / pltpu.roll(a0, DH, 1), a1 / pltpu.roll(a1, DH, 1))
        o_ref[0, :, cols] = (o * _silu(g_ref[0, :, cols].astype(F32))).astype(BF16)


def _single_pass_head(s_c, s_n, vc, vn):
    m = jnp.maximum(jnp.max(s_c, axis=1, keepdims=True), jnp.max(s_n, axis=1, keepdims=True))
    p_c = jnp.exp2(s_c - m).astype(BF16)
    p_n = jnp.exp2(s_n - m).astype(BF16)
    return (jnp.dot(p_c, vc, preferred_element_type=F32) + jnp.dot(p_n, vn, preferred_element_type=F32))


def _stack_pair_queries(q_ref, qm_sc, n_pairs, tq):
    for p in range(n_pairs):
        qp = q_ref[0, :, p * LANES:(p + 1) * LANES]
        qm_sc[p, 0:tq, :] = _head_of_pair(qp, 0)
        qm_sc[p, tq:2 * tq, :] = _head_of_pair(qp, 1)


def _values_with_ones(v):
    return jnp.concatenate([v, jnp.ones_like(v)], axis=1)


def _finish_stacked_pairs(acc_sc, g_ref, o_ref, n_pairs, tq):
    lane = _lane_iota((tq, LANES))
    for p in range(n_pairs):
        cols = slice(p * LANES, (p + 1) * LANES)
        a0 = acc_sc[p, 0:tq, :]
        a1 = acc_sc[p, tq:2 * tq, :]
        o = jnp.where(lane < DH, a0[:, 0:LANES] / a0[:, LANES:2 * LANES], a1[:, 0:LANES] / a1[:, LANES:2 * LANES])
        o_ref[0, :, cols] = (o * _silu(g_ref[0, :, cols].astype(F32))).astype(BF16)


def _fox_kernel(q_ref, k_ref, v_ref, g_ref, fq_ref, fk_ref, o_ref, qm_sc, fq_sc, m_sc, acc_sc, *, tq, tk):
    qi = pl.program_id(1)
    n_full = (qi * tq) // tk
    n_pairs = H_A // 2
    _stack_pair_queries(q_ref, qm_sc, n_pairs, tq)
    for p in range(n_pairs):
        for hh in range(2):
            h = 2 * p + hh
            fq_sc[p, hh * tq:(hh + 1) * tq, :] = jnp.broadcast_to(fq_ref[0, :, h:h + 1], (tq, LANES))
    m_sc[...] = jnp.full(m_sc.shape, NEG_INF, F32)
    acc_sc[...] = jnp.zeros(acc_sc.shape, F32)
    n_rep = tk // LANES

    def block(j, diagonal):
        k0 = pl.multiple_of(j * tk, tk)
        if diagonal:
            keep = (k0 + lax.broadcasted_iota(I32, (1, tq, tk), 2)) <= (qi * tq + lax.broadcasted_iota(I32, (1, tq, tk), 1))
        for p in range(n_pairs):
            cols = slice(p * LANES, (p + 1) * LANES)
            s = _qk(qm_sc[p], k_ref[0, pl.ds(k0, tk), cols]) + _repeat_lanes(fq_sc[p], n_rep)
            s = s.reshape(2, tq, tk) - fk_ref[0, j, 2 * p:2 * p + 2, :][:, None, :]
            if diagonal:
                s = jnp.where(keep, s, NEG_INF)
            _flash_update(p, s.reshape(2 * tq, tk), _values_with_ones(v_ref[0, pl.ds(k0, tk), cols]), m_sc, acc_sc)

    def body(j, carry):
        block(j, False)
        return carry

    lax.fori_loop(0, n_full, body, 0)
    block(n_full, True)
    _finish_stacked_pairs(acc_sc, g_ref, o_ref, n_pairs, tq)


def _fox_call(zb3, fq, fk, *, b, t, tq, tk):
    nkb = t // tk
    return pl.pallas_call(
        functools.partial(_fox_kernel, tq=tq, tk=tk),
        out_shape=jax.ShapeDtypeStruct((b, t, HW), BF16),
        grid=(b, t // tq),
        in_specs=[pl.BlockSpec((1, tq, HW), lambda i, j: (i, j, QA // HW)),
                  pl.BlockSpec((1, t, HW), lambda i, j: (i, 0, KA // HW)),
                  pl.BlockSpec((1, t, HW), lambda i, j: (i, 0, VA // HW)),
                  pl.BlockSpec((1, tq, HW), lambda i, j: (i, j, GA // HW)),
                  pl.BlockSpec((1, tq, H_A), lambda i, j: (i, j, 0)),
                  pl.BlockSpec((1, nkb, H_A, tk), lambda i, j: (i, 0, 0, 0))],
        out_specs=pl.BlockSpec((1, tq, HW), lambda i, j: (i, j, 0)),
        scratch_shapes=[pltpu.VMEM((H_A // 2, 2 * tq, LANES), BF16), pltpu.VMEM((H_A // 2, 2 * tq, LANES), F32),
                        pltpu.VMEM((H_A // 2, 2 * tq, LANES), F32), pltpu.VMEM((H_A // 2, 2 * tq, 2 * LANES), F32)],
        compiler_params=_cparams(2),
        name="fox",
    )(zb3, zb3, zb3, zb3, fq, fk)


def _fox_s_kernel(q_ref, g_ref, kn_ref, vn_ref, kc_ref, vc_ref, fq_ref, fk_ref, o_ref, *, t, past):
    qpos = past + lax.broadcasted_iota(I32, (t, 1), 0)
    keep_n = (past + lax.broadcasted_iota(I32, (1, NEW_BLOCK), 1)) <= qpos
    accs = []
    for p in range(H_A // 2):
        cols = slice(p * LANES, (p + 1) * LANES)
        kc = kc_ref[0, :, cols].astype(BF16)
        vc = vc_ref[0, :, cols].astype(BF16)
        kn = kn_ref[0, :, cols]
        vn = vn_ref[0, :, cols]
        qp = q_ref[0, :, cols]
        for hh in range(2):
            h = 2 * p + hh
            qh = _head_of_pair(qp, hh)
            fq = fq_ref[0, :, h:h + 1]
            s_c = _qk(qh, kc) + fq - fk_ref[0, h:h + 1, 0:past]
            s_n = _qk(qh, kn) + fq - fk_ref[0, h:h + 1, past:past + NEW_BLOCK]
            s_n = jnp.where(keep_n, s_n, NEG_INF)
            accs.append(_single_pass_head(s_c, s_n, _ones_for_other_head(vc, hh), _ones_for_other_head(vn, hh)))
    _finish_pairs(lambda h: accs[h], g_ref, o_ref, H_A // 2, t)


def _fox_s_call(zb3, kn, vn, kc, vc, fq, cum_t, *, b, t, past):
    tk_total = past + NEW_BLOCK
    return pl.pallas_call(
        functools.partial(_fox_s_kernel, t=t, past=past),
        out_shape=jax.ShapeDtypeStruct((b, t, HW), BF16),
        grid=(b,),
        in_specs=[pl.BlockSpec((1, t, HW), lambda i: (i, 0, QA // HW)),
                  pl.BlockSpec((1, t, HW), lambda i: (i, 0, GA // HW)),
                  pl.BlockSpec((1, NEW_BLOCK, HW), lambda i: (i, 0, 0)),
                  pl.BlockSpec((1, NEW_BLOCK, HW), lambda i: (i, 0, 0)),
                  pl.BlockSpec((1, past, HW), lambda i: (i, 0, 0)),
                  pl.BlockSpec((1, past, HW), lambda i: (i, 0, 0)),
                  pl.BlockSpec((1, t, H_A), lambda i: (i, 0, 0)),
                  pl.BlockSpec((1, H_A, tk_total), lambda i: (i, 0, 0))],
        out_specs=pl.BlockSpec((1, t, HW), lambda i: (i, 0, 0)),
        compiler_params=_cparams(1),
        name="fox_s",
    )(zb3, zb3, kn, vn, kc, vc, fq, cum_t)


def _float_keys(score):
    bits = lax.bitcast_convert_type(score, I32)
    key = bits ^ ((bits >> 31) & jnp.int32(0x7FFFFFFF))
    return jnp.where(bits == INT_MIN, 0, key)


def _score_keys(iq_of_pair, iw_of_head, ikb, adm, rows):
    acc = jnp.zeros((rows, ikb.shape[0]), F32)
    for p in range(H_IDX // 2):
        iqp = iq_of_pair(p)
        for hh in range(2):
            x = _qk(_head_of_pair(iqp, hh), ikb)
            acc = acc + iw_of_head(2 * p + hh) * jnp.maximum(x, 0.0)
    return _float_keys(jnp.where(adm, acc, NEG_INF))


def _row_totals(acc):
    ones = jnp.ones((LANES, LANES), BF16)
    return jnp.dot(acc.astype(F32).astype(BF16), ones, preferred_element_type=F32)


def _split_key(key):
    return (key >> 16).astype(I16), ((key & 0xFFFF) - 32768).astype(I16)


def _count16(sc, n_blocks, r0, rows, pred):
    acc = jnp.zeros((rows, LANES), I16)
    one, zero = jnp.int16(1), jnp.int16(0)
    for j in range(n_blocks):
        for c in range(0, sc.shape[2], LANES):
            acc = acc + jnp.where(pred(sc[j, r0:r0 + rows, c:c + LANES]), one, zero)
    return _row_totals(acc)


def _bisect16(sc, n_blocks, rows, targets):
    chains = len(targets)
    one, zero = jnp.int16(1), jnp.int16(0)
    bits = 2 if n_blocks * sc.shape[2] <= BISECT_TWO_BIT_MAX_KEYS else 1
    n_cand = 2 ** bits - 1

    def step(anss, inc):
        for r in range(chains):
            cands = [anss[r] + inc * m for m in range(1, n_cand + 1)]
            c16 = [c.astype(I16) for c in cands]
            accs = [jnp.zeros((rows, LANES), I16) for _ in c16]
            for j in range(n_blocks):
                for c in range(0, sc.shape[2], LANES):
                    kb = sc[j, r * rows:(r + 1) * rows, c:c + LANES]
                    accs = [a + jnp.where(kb >= cc, one, zero) for a, cc in zip(accs, c16)]
            tot = _row_totals(accs[0] if n_cand == 1 else jnp.concatenate(accs, axis=0))
            for m in range(n_cand):
                anss[r] = jnp.where(tot[m * rows:(m + 1) * rows] >= targets[r], cands[m], anss[r])
        return anss

    steps_per_trip = BISECT_BITS_PER_TRIP // bits

    def trip(it, anss):
        anss = list(anss)
        for b in range(steps_per_trip):
            anss = step(anss, lax.shift_left(jnp.int32(1), 16 - bits * (it * steps_per_trip + b + 1)))
        return tuple(anss)

    init = tuple(jnp.full((rows, LANES), I16_MIN, I32) for _ in range(chains))
    return lax.fori_loop(0, 16 // BISECT_BITS_PER_TRIP, trip, init)


def _topk_threshold(hi_sc, lo_sc, n_blocks, n_sel, thr_sc, need_sc):
    total_rows = hi_sc.shape[1]
    rows = min(BISECT_CHAIN_ROWS, total_rows)
    chains = total_rows // rows
    thr_hi = _bisect16(hi_sc, n_blocks, rows, [n_sel] * chains)
    above_hi = []
    for r in range(chains):
        t16 = thr_hi[r].astype(I16)
        above_hi.append(_count16(hi_sc, n_blocks, r * rows, rows, lambda kb, t=t16: kb > t))
        for j in range(n_blocks):
            for c in range(0, hi_sc.shape[2], LANES):
                sl = (j, slice(r * rows, (r + 1) * rows), slice(c, c + LANES))
                lo_sc[sl] = jnp.where(hi_sc[sl] == t16, lo_sc[sl], jnp.int16(I16_MIN))
    thr_lo = _bisect16(lo_sc, n_blocks, rows, [n_sel - a for a in above_hi])
    for r in range(chains):
        t16 = thr_lo[r].astype(I16)
        above = above_hi[r] + _count16(lo_sc, n_blocks, r * rows, rows, lambda kb, t=t16: kb > t)
        thr_sc[r * rows:(r + 1) * rows, :] = thr_hi[r] * 65536 + (thr_lo[r] + 32768)
        need_sc[r * rows:(r + 1) * rows, :] = n_sel - above


def _select_block(kb, thr, need, seen):
    sub = min(kb.shape[1], SELECT_WIDTH)
    n = sub // LANES
    thr_w = _repeat_lanes(thr, n)
    need_w = _repeat_lanes(need, n)
    upper = lax.broadcasted_iota(I32, (sub, sub), 0) <= lax.broadcasted_iota(I32, (sub, sub), 1)
    upper = jnp.where(upper, 1.0, 0.0).astype(BF16)
    ones = jnp.ones((sub, LANES), BF16)
    out = []
    for c in range(0, kb.shape[1], sub):
        kc = kb[:, c:c + sub]
        tie = jnp.where(kc == thr_w, 1.0, 0.0)
        tie_b = tie.astype(BF16)
        rank = _repeat_lanes(seen, n) + jnp.dot(tie_b, upper, preferred_element_type=F32)
        take = jnp.where(kc > thr_w, 1.0, jnp.where(rank <= need_w, tie, 0.0))
        take = jnp.where(kc == KEY_MASKED, 0.0, take)
        seen = seen + jnp.dot(tie_b, ones, preferred_element_type=F32)
        out.append(jnp.where(take > 0.5, 0.0, NEG_INF))
    return (out[0] if len(out) == 1 else jnp.concatenate(out, axis=1)), seen


def _dsa_kernel(q_ref, iq_ref, sm_ref, g_ref, k_ref, v_ref, ik_ref, o_ref,
                key_sc, hi_sc, lo_sc, bias_sc, thr_sc, need_sc, iqm_sc, qm_sc, m_sc, acc_sc,
                *, tq, tk, n_sel, nkb_max):
    qi = pl.program_id(1)
    nkb = ((qi + 1) * tq + tk - 1) // tk
    qchunk = (qi * tq + lax.broadcasted_iota(I32, (tq, 1), 0)) >> CHUNK_SHIFT
    kidx = lax.broadcasted_iota(I32, (1, tk), 1)
    n_pairs = H_B // 2

    for p in range(H_IDX // 2):
        iqp = iq_ref[0, :, p * LANES:(p + 1) * LANES]
        for hh in range(2):
            iqm_sc[(2 * p + hh) * tq:(2 * p + hh + 1) * tq, :] = _head_of_pair(iqp, hh)
    group = H_IDX // IDX_HEAD_GROUPS

    def score_body(j, carry):
        k0 = pl.multiple_of(j * tk, tk)
        ikb = ik_ref[0, pl.ds(k0, tk), :]
        acc = jnp.zeros((tq, tk), F32)
        for gi in range(IDX_HEAD_GROUPS):
            x = _qk(iqm_sc[gi * group * tq:(gi + 1) * group * tq, :], ikb)
            for hg in range(group):
                h = gi * group + hg
                acc = acc + sm_ref[0, :, SM_IW + h:SM_IW + h + 1] * jnp.maximum(x[hg * tq:(hg + 1) * tq], 0.0)
        adm = ((k0 + kidx) >> CHUNK_SHIFT) <= qchunk
        key = _float_keys(jnp.where(adm, acc, NEG_INF))
        key_sc[j] = key
        hi_sc[j], lo_sc[j] = _split_key(key)
        return carry

    lax.fori_loop(0, nkb, score_body, 0)

    for nv in range(1, nkb_max + 1):
        @pl.when(nkb == nv)
        def _(nv=nv):
            _topk_threshold(hi_sc, lo_sc, nv, n_sel, thr_sc, need_sc)

    def sel_body(j, seen):
        bias, seen = _select_block(key_sc[j], thr_sc[...], need_sc[...], seen)
        bias_sc[j] = bias
        return seen

    lax.fori_loop(0, nkb, sel_body, jnp.zeros((tq, LANES), F32))

    _stack_pair_queries(q_ref, qm_sc, n_pairs, tq)
    m_sc[...] = jnp.full(m_sc.shape, NEG_INF, F32)
    acc_sc[...] = jnp.zeros(acc_sc.shape, F32)

    def att_body(j, carry):
        k0 = pl.multiple_of(j * tk, tk)
        bias = bias_sc[j][None]
        for p in range(n_pairs):
            cols = slice(p * LANES, (p + 1) * LANES)
            s = _qk(qm_sc[p], k_ref[0, pl.ds(k0, tk), cols]).reshape(2, tq, tk) + bias
            _flash_update(p, s.reshape(2 * tq, tk), _values_with_ones(v_ref[0, pl.ds(k0, tk), cols]), m_sc, acc_sc)
        return carry

    lax.fori_loop(0, nkb, att_body, 0)
    _finish_stacked_pairs(acc_sc, g_ref, o_ref, n_pairs, tq)


def _dsa_call(zb3, sm3, *, b, t, tq, tk, n_sel):
    nkb_max = t // tk
    kern = functools.partial(_dsa_kernel, tq=tq, tk=tk, n_sel=n_sel, nkb_max=nkb_max)
    return pl.pallas_call(
        kern,
        out_shape=jax.ShapeDtypeStruct((b, t, HW), BF16),
        grid=(b, t // tq),
        in_specs=[pl.BlockSpec((1, tq, HW), lambda i, j: (i, j, QB // HW)),
                  pl.BlockSpec((1, tq, HW), lambda i, j: (i, j, IQ // HW)),
                  pl.BlockSpec((1, tq, LANES), lambda i, j: (i, j, 0)),
                  pl.BlockSpec((1, tq, HW), lambda i, j: (i, j, GB // HW)),
                  pl.BlockSpec((1, t, HW), lambda i, j: (i, 0, KB // HW)),
                  pl.BlockSpec((1, t, HW), lambda i, j: (i, 0, VB // HW)),
                  pl.BlockSpec((1, t, LANES), lambda i, j: (i, 0, SM // LANES))],
        out_specs=pl.BlockSpec((1, tq, HW), lambda i, j: (i, j, 0)),
        scratch_shapes=[pltpu.VMEM((nkb_max, tq, tk), I32), pltpu.VMEM((nkb_max, tq, tk), I16),
                        pltpu.VMEM((nkb_max, tq, tk), I16), pltpu.VMEM((nkb_max, tq, tk), F32),
                        pltpu.VMEM((tq, LANES), I32), pltpu.VMEM((tq, LANES), F32),
                        pltpu.VMEM((H_IDX * tq, LANES), BF16),
                        pltpu.VMEM((H_B // 2, 2 * tq, LANES), BF16), pltpu.VMEM((H_B // 2, 2 * tq, LANES), F32),
                        pltpu.VMEM((H_B // 2, 2 * tq, 2 * LANES), F32)],
        compiler_params=_cparams(2),
        name="dsa",
    )(zb3, zb3, sm3, zb3, zb3, zb3, zb3)


def _dsa_sel_s_kernel(iq_ref, sm_ref, ikc_ref, ikn_ref, bias_ref, key_sc, hi_sc, lo_sc, thr_sc, need_sc,
                      *, bt, t, past, n_sel):
    n_blocks = past // LANES + 1
    kpos_n = past + lax.broadcasted_iota(I32, (1, NEW_BLOCK), 1)
    for b in range(bt):
        rows = slice(b * t, (b + 1) * t)
        iq_of_pair = lambda p, b=b: iq_ref[b, :, p * LANES:(p + 1) * LANES]
        iw_of_head = lambda h, b=b: sm_ref[b, :, SM_IW + h:SM_IW + h + 1]
        for j in range(n_blocks):
            if j < n_blocks - 1:
                key = _score_keys(iq_of_pair, iw_of_head, ikc_ref[b, j * LANES:(j + 1) * LANES, :], True, t)
            else:
                key = _score_keys(iq_of_pair, iw_of_head, ikn_ref[b], kpos_n < past + t, t)
            key_sc[j, rows, :] = key
            hi_sc[j, rows, :], lo_sc[j, rows, :] = _split_key(key)
    _topk_threshold(hi_sc, lo_sc, n_blocks, n_sel, thr_sc, need_sc)
    seen = jnp.zeros((bt * t, LANES), F32)
    for j in range(n_blocks):
        bias, seen = _select_block(key_sc[j], thr_sc[...], need_sc[...], seen)
        for b in range(bt):
            bias_ref[b, :, j * LANES:(j + 1) * LANES] = bias[b * t:(b + 1) * t]


def _dsa_sel_s_call(zb3, sm3, ikc, ikn, *, b, t, past, n_sel, bt):
    tk_total = past + NEW_BLOCK
    n_blocks = tk_total // LANES
    return pl.pallas_call(
        functools.partial(_dsa_sel_s_kernel, bt=bt, t=t, past=past, n_sel=n_sel),
        out_shape=jax.ShapeDtypeStruct((b, t, tk_total), F32),
        grid=(b // bt,),
        in_specs=[pl.BlockSpec((bt, t, HW), lambda i: (i, 0, IQ // HW)),
                  pl.BlockSpec((bt, t, LANES), lambda i: (i, 0, 0)),
                  pl.BlockSpec((bt, past, LANES), lambda i: (i, 0, 0)),
                  pl.BlockSpec((bt, NEW_BLOCK, LANES), lambda i: (i, 0, 0))],
        out_specs=pl.BlockSpec((bt, t, tk_total), lambda i: (i, 0, 0)),
        scratch_shapes=[pltpu.VMEM((n_blocks, bt * t, LANES), I32), pltpu.VMEM((n_blocks, bt * t, LANES), I16),
                        pltpu.VMEM((n_blocks, bt * t, LANES), I16),
                        pltpu.VMEM((bt * t, LANES), I32), pltpu.VMEM((bt * t, LANES), F32)],
        compiler_params=_cparams(1),
        name="dsa_sel_s",
    )(zb3, sm3, ikc, ikn)


def _dsa_att_s_kernel(q_ref, g_ref, kn_ref, vn_ref, kc_ref, vc_ref, bias_ref, o_ref, *, t, past):
    bias_c = bias_ref[0, :, 0:past]
    bias_n = bias_ref[0, :, past:past + NEW_BLOCK]
    accs = []
    for p in range(H_B // 2):
        cols = slice(p * LANES, (p + 1) * LANES)
        kc = kc_ref[0, :, cols].astype(BF16)
        vc = vc_ref[0, :, cols].astype(BF16)
        kn = kn_ref[0, :, cols]
        vn = vn_ref[0, :, cols]
        qp = q_ref[0, :, cols]
        for hh in range(2):
            qh = _head_of_pair(qp, hh)
            accs.append(_single_pass_head(_qk(qh, kc) + bias_c, _qk(qh, kn) + bias_n,
                                          _ones_for_other_head(vc, hh), _ones_for_other_head(vn, hh)))
    _finish_pairs(lambda h: accs[h], g_ref, o_ref, H_B // 2, t)


def _dsa_att_s_call(zb3, kn, vn, kc, vc, bias, *, b, t, past):
    tk_total = past + NEW_BLOCK
    return pl.pallas_call(
        functools.partial(_dsa_att_s_kernel, t=t, past=past),
        out_shape=jax.ShapeDtypeStruct((b, t, HW), BF16),
        grid=(b,),
        in_specs=[pl.BlockSpec((1, t, HW), lambda i: (i, 0, QB // HW)),
                  pl.BlockSpec((1, t, HW), lambda i: (i, 0, GB // HW)),
                  pl.BlockSpec((1, NEW_BLOCK, HW), lambda i: (i, 0, 0)),
                  pl.BlockSpec((1, NEW_BLOCK, HW), lambda i: (i, 0, 0)),
                  pl.BlockSpec((1, past, HW), lambda i: (i, 0, 0)),
                  pl.BlockSpec((1, past, HW), lambda i: (i, 0, 0)),
                  pl.BlockSpec((1, t, tk_total), lambda i: (i, 0, 0))],
        out_specs=pl.BlockSpec((1, t, HW), lambda i: (i, 0, 0)),
        compiler_params=_cparams(1),
        name="dsa_att_s",
    )(zb3, zb3, kn, vn, kc, vc, bias)


def _mid_kernel(x_ref, ma_ref, mb_ref, wo_ref, mod0_ref, mod1_ref, ng_ref, wc_ref, qg_ref, kvg_ref,
                wuq_ref, wuk_ref, cos_ref, sin_ref,
                x1_ref, g_ref, qlat_ref, qrope_ref, kcat_ref, clat_ref, ckr_ref, *, bt, tt):
    y = (jnp.dot(ma_ref[...], wo_ref[0:HW, :], preferred_element_type=F32)
         + jnp.dot(mb_ref[...], wo_ref[HW:2 * HW, :], preferred_element_type=F32))
    x1 = _gated_residual(x_ref[...], y, mod0_ref, bt, tt)
    x1_ref[...] = x1
    h = _norm_mod(x1, mod1_ref, ng_ref[...], bt, tt).astype(BF16)
    cos = cos_ref[...]
    sin = sin_ref[...]

    def proj(c0, n):
        return jnp.dot(h, wc_ref[:, c0:c0 + n], preferred_element_type=F32)

    g_ref[...] = proj(C_G, H_C * DV).astype(BF16)
    cq = _rms(proj(C_Q, Q_LORA), qg_ref[...]).astype(BF16)
    n_nope = H_C * NOPE
    q_nope = jnp.dot(cq, wuq_ref[:, 0:n_nope], preferred_element_type=F32).astype(BF16)
    for j in range(H_C // 2):
        ql = jnp.dot(q_nope[:, j * LANES:(j + 1) * LANES], wuk_ref[j], preferred_element_type=F32)
        qlat_ref[:, 2 * j * KV_LORA:2 * (j + 1) * KV_LORA] = (ql * C_SCALE).astype(BF16)
    q_rope = jnp.dot(cq, wuq_ref[:, n_nope:], preferred_element_type=F32)
    for c in range(0, H_C * ROPE_DIM, LANES):
        qrope_ref[:, c:c + LANES] = (_rope(q_rope[:, c:c + LANES], cos, sin, ROPE_DIM) * C_SCALE).astype(BF16)
    ckv = _rms(proj(C_KV, KV_LORA), kvg_ref[...])
    clat_ref[...] = ckv
    kr = _rope(proj(C_KR, LANES), cos, sin, ROPE_DIM)
    ckr_ref[...] = kr
    kr = jnp.where(_lane_iota(kr.shape) < ROPE_DIM, kr, 0.0)
    kr4 = kr + pltpu.roll(kr, ROPE_DIM, 1) + pltpu.roll(kr, 2 * ROPE_DIM, 1) + pltpu.roll(kr, 3 * ROPE_DIM, 1)
    kcat_ref[:, 0:KV_LORA] = ckv.astype(BF16)
    kcat_ref[:, KV_LORA:KV_LORA + LANES] = kr4.astype(BF16)


def _mid_call(x2, ma, mb, wo, mod0, mod1, ng, wc, qg, kvg, wuq, wuk_bd, cos, sin, *, t, bt, tt):
    n, d = x2.shape
    tm = bt * tt
    tpb = t // tt
    row = lambda i: (i, 0)
    full2 = lambda i: (0, 0)
    modspec = pl.BlockSpec((bt, 3, d), lambda i: (i * tt // t, 0, 0))
    nq = H_C * KV_LORA
    return pl.pallas_call(
        functools.partial(_mid_kernel, bt=bt, tt=tt),
        out_shape=(jax.ShapeDtypeStruct((n, d), F32),
                   jax.ShapeDtypeStruct((n, H_C * DV), BF16),
                   jax.ShapeDtypeStruct((n, nq), BF16),
                   jax.ShapeDtypeStruct((n, H_C * ROPE_DIM), BF16),
                   jax.ShapeDtypeStruct((n, 2 * LANES), BF16),
                   jax.ShapeDtypeStruct((n, KV_LORA), F32),
                   jax.ShapeDtypeStruct((n, LANES), F32)),
        grid=(n // tm,),
        in_specs=[pl.BlockSpec((tm, d), row),
                  pl.BlockSpec((tm, HW), row), pl.BlockSpec((tm, HW), row),
                  pl.BlockSpec((2 * HW, d), full2),
                  modspec, modspec,
                  pl.BlockSpec((1, d), full2),
                  pl.BlockSpec((d, CW), full2),
                  pl.BlockSpec((1, Q_LORA), full2), pl.BlockSpec((1, KV_LORA), full2),
                  pl.BlockSpec(wuq.shape, full2),
                  pl.BlockSpec(wuk_bd.shape, lambda i: (0, 0, 0)),
                  pl.BlockSpec((tm, LANES), lambda i: (i % tpb, 0)),
                  pl.BlockSpec((tm, LANES), lambda i: (i % tpb, 0))],
        out_specs=(pl.BlockSpec((tm, d), row), pl.BlockSpec((tm, H_C * DV), row),
                   pl.BlockSpec((tm, nq), row), pl.BlockSpec((tm, H_C * ROPE_DIM), row),
                   pl.BlockSpec((tm, 2 * LANES), row), pl.BlockSpec((tm, KV_LORA), row),
                   pl.BlockSpec((tm, LANES), row)),
        compiler_params=_cparams(1),
        name="mid",
    )(x2, ma, mb, wo, mod0, mod1, ng, wc, qg, kvg, wuq, wuk_bd, cos, sin)


def _stack_heads(qlat_ref, qrope_ref, qs_sc, b, tq):
    lane = _lane_iota((tq, LANES))
    heads_per_block = LANES // ROPE_DIM
    for h in range(H_C):
        qs_sc[h * tq:(h + 1) * tq, 0:KV_LORA] = qlat_ref[b, :, h * KV_LORA:(h + 1) * KV_LORA]
        blk = qrope_ref[b, :, (h // heads_per_block) * LANES:(h // heads_per_block + 1) * LANES]
        mine = (lane >> 5) == (h % heads_per_block)
        qs_sc[h * tq:(h + 1) * tq, KV_LORA:KV_LORA + LANES] = jnp.where(mine, blk, jnp.zeros_like(blk))


def _latent_with_ones(lat):
    return jnp.concatenate([lat, jnp.ones_like(lat)], axis=1)


def _mla_kernel(qlat_ref, qrope_ref, k_ref, o_ref, qs_sc, m_sc, acc_sc, *, tq, tk, nkb_max):
    qi = pl.program_id(1)
    rows = H_C * tq
    _stack_heads(qlat_ref, qrope_ref, qs_sc, 0, tq)
    m_sc[...] = jnp.full(m_sc.shape, NEG_INF, F32)
    acc_sc[...] = jnp.zeros(acc_sc.shape, F32)
    n_full = (qi * tq) // tk
    qchunk = (qi * tq + lax.broadcasted_iota(I32, (1, tq, 1), 1)) >> CHUNK_SHIFT
    kidx = lax.broadcasted_iota(I32, (1, 1, tk), 2)

    for nv in range(1, nkb_max + 1):
        @pl.when(n_full + 1 == nv)
        def _(nv=nv):
            s = _qk(qs_sc[...], k_ref[0, 0:tk, :])
            for j in range(nv):
                kb = k_ref[0, j * tk:(j + 1) * tk, :]
                s_next = _qk(qs_sc[...], k_ref[0, (j + 1) * tk:(j + 2) * tk, :]) if j + 1 < nv else None
                if j == nv - 1:
                    adm = ((j * tk + kidx) >> CHUNK_SHIFT) <= qchunk
                    s = jnp.where(adm, s.reshape(H_C, tq, tk), NEG_INF).reshape(rows, tk)
                _flash_update(0, s, _latent_with_ones(kb[:, 0:KV_LORA]), m_sc, acc_sc)
                s = s_next

    acc = acc_sc[0]
    o = (acc[:, 0:KV_LORA] / acc[:, KV_LORA:2 * KV_LORA]).astype(BF16)
    for h in range(H_C):
        o_ref[0, :, h * KV_LORA:(h + 1) * KV_LORA] = o[h * tq:(h + 1) * tq]


def _mla_call(qlat, qrope, kcat, *, b, t, tq, tk):
    nq = H_C * KV_LORA
    rows = H_C * tq
    return pl.pallas_call(
        functools.partial(_mla_kernel, tq=tq, tk=tk, nkb_max=t // tk),
        out_shape=jax.ShapeDtypeStruct((b, t, nq), BF16),
        grid=(b, t // tq),
        in_specs=[pl.BlockSpec((1, tq, nq), lambda i, j: (i, j, 0)),
                  pl.BlockSpec((1, tq, H_C * ROPE_DIM), lambda i, j: (i, j, 0)),
                  pl.BlockSpec((1, t, 2 * LANES), lambda i, j: (i, 0, 0))],
        out_specs=pl.BlockSpec((1, tq, nq), lambda i, j: (i, j, 0)),
        scratch_shapes=[pltpu.VMEM((rows, 2 * LANES), BF16),
                        pltpu.VMEM((1, rows, LANES), F32), pltpu.VMEM((1, rows, 2 * LANES), F32)],
        compiler_params=_cparams(2),
        name="mla",
    )(qlat, qrope, kcat)


def _mla_s_kernel(qlat_ref, qrope_ref, latc_ref, krc_ref, kn_ref, o_ref, qs_sc, *, bt, t, past):
    keep_n = (past + lax.broadcasted_iota(I32, (1, NEW_BLOCK), 1)) < past + t
    for b in range(bt):
        _stack_heads(qlat_ref, qrope_ref, qs_sc.at[b], b, t)
        q = qs_sc[b]
        lat = latc_ref[b].astype(BF16)
        kc = jnp.concatenate([lat, krc_ref[b]], axis=1)
        kn = kn_ref[b]
        s_c = _qk(q, kc)
        s_n = jnp.where(keep_n, _qk(q, kn), NEG_INF)
        acc = _single_pass_head(s_c, s_n, _latent_with_ones(lat), _latent_with_ones(kn[:, 0:KV_LORA]))
        o = (acc[:, 0:KV_LORA] / acc[:, KV_LORA:2 * KV_LORA]).astype(BF16)
        for h in range(H_C):
            o_ref[b, :, h * KV_LORA:(h + 1) * KV_LORA] = o[h * t:(h + 1) * t]


def _mla_s_call(qlat, qrope, latc, krc, kn, *, b, t, past, bt):
    nq = H_C * KV_LORA
    return pl.pallas_call(
        functools.partial(_mla_s_kernel, bt=bt, t=t, past=past),
        out_shape=jax.ShapeDtypeStruct((b, t, nq), BF16),
        grid=(b // bt,),
        in_specs=[pl.BlockSpec((bt, t, nq), lambda i: (i, 0, 0)),
                  pl.BlockSpec((bt, t, H_C * ROPE_DIM), lambda i: (i, 0, 0)),
                  pl.BlockSpec((bt, past, KV_LORA), lambda i: (i, 0, 0)),
                  pl.BlockSpec((bt, past, LANES), lambda i: (i, 0, 0)),
                  pl.BlockSpec((bt, NEW_BLOCK, 2 * LANES), lambda i: (i, 0, 0))],
        out_specs=pl.BlockSpec((bt, t, nq), lambda i: (i, 0, 0)),
        scratch_shapes=[pltpu.VMEM((bt, H_C * t, 2 * LANES), BF16)],
        compiler_params=_cparams(1),
        name="mla_s",
    )(qlat, qrope, latc, krc, kn)


def _out_c_kernel(x1_ref, ol_ref, g_ref, wuv_ref, wo_ref, mod_ref, fg_ref, y_ref, *, bt, tt):
    o = jnp.concatenate(
        [jnp.dot(ol_ref[:, 2 * j * KV_LORA:2 * (j + 1) * KV_LORA], wuv_ref[j], preferred_element_type=F32)
         for j in range(H_C // 2)], axis=1)
    mixed = (o * _silu(g_ref[...].astype(F32))).astype(BF16)
    y = jnp.dot(mixed, wo_ref[...], preferred_element_type=F32)
    x2 = _gated_residual(x1_ref[...], y, mod_ref, bt, tt)
    y_ref[...] = _rms(x2, fg_ref[...])


def _out_c_call(x1, olat, g, wuv_bd, wo, mod1, fg, *, t, bt, tt):
    n, d = x1.shape
    tm = bt * tt
    row = lambda i: (i, 0)
    full2 = lambda i: (0, 0)
    return pl.pallas_call(
        functools.partial(_out_c_kernel, bt=bt, tt=tt),
        out_shape=jax.ShapeDtypeStruct((n, d), F32),
        grid=(n // tm,),
        in_specs=[pl.BlockSpec((tm, d), row),
                  pl.BlockSpec((tm, H_C * KV_LORA), row),
                  pl.BlockSpec((tm, H_C * DV), row),
                  pl.BlockSpec(wuv_bd.shape, lambda i: (0, 0, 0)),
                  pl.BlockSpec(wo.shape, full2),
                  pl.BlockSpec((bt, 3, d), lambda i: (i * tt // t, 0, 0)),
                  pl.BlockSpec((1, d), full2)],
        out_specs=pl.BlockSpec((tm, d), row),
        compiler_params=_cparams(1),
        name="out_c",
    )(x1, olat, g, wuv_bd, wo, mod1, fg)


def _rope_tables(pos, dim, reps_rows):
    half = dim // 2
    inv_freq = ROPE_THETA ** (-jnp.arange(half, dtype=F32) / half)
    ang = pos.astype(F32)[:, None] * inv_freq[None, :]
    cos = jnp.cos(ang)
    sin = jnp.sin(ang)
    cos_d = jnp.concatenate([cos, cos], axis=-1)
    sin_d = jnp.concatenate([-sin, sin], axis=-1)
    reps = LANES // dim
    return jnp.tile(cos_d, (reps_rows, reps)), jnp.tile(sin_d, (reps_rows, reps))


def _prep_weights(ab_w_in, ab_f_bias, ab_w_out, c_w_in, c_w_uq, c_w_uk, c_w_uv, c_w_out):
    d = D_MODEL
    w = ab_w_in
    offs = {}
    start = 0
    for name, size in (("qa", HW), ("ka", HW), ("va", HW), ("fa", H_A), ("ga", HW), ("qb", HW), ("kb", HW),
                       ("vb", HW), ("iq", HW), ("ik", DH), ("iw", H_IDX), ("gb", HW)):
        offs[name] = w[:, start:start + size]
        start += size
    pad = jnp.zeros((d, LANES - DH - H_A - H_IDX), w.dtype)
    w_ab = jnp.concatenate([offs[k] for k in ("qa", "ka", "va", "ga", "qb", "kb", "vb", "iq", "gb",
                                              "ik", "fa", "iw")] + [pad], axis=1).astype(BF16)
    fb = jnp.zeros((1, LANES), F32).at[0, SM_FA:SM_FA + H_A].set(ab_f_bias)

    cq, ckv, kr, g = (c_w_in[:, 0:Q_LORA], c_w_in[:, Q_LORA:Q_LORA + KV_LORA],
                      c_w_in[:, Q_LORA + KV_LORA:Q_LORA + KV_LORA + ROPE_DIM],
                      c_w_in[:, Q_LORA + KV_LORA + ROPE_DIM:])
    w_c = jnp.concatenate([g, cq, ckv, kr, jnp.zeros((d, LANES - ROPE_DIM), c_w_in.dtype)], axis=1).astype(BF16)
    uq = c_w_uq.reshape(Q_LORA, H_C, NOPE + ROPE_DIM)
    w_uq = jnp.concatenate([uq[:, :, :NOPE].reshape(Q_LORA, H_C * NOPE),
                            uq[:, :, NOPE:].reshape(Q_LORA, H_C * ROPE_DIM)], axis=1).astype(BF16)
    uk = jnp.transpose(c_w_uk, (1, 2, 0)).reshape(H_C // 2, 2, NOPE, KV_LORA)
    zk = jnp.zeros_like(uk[:, 0])
    wuk_bd = jnp.concatenate([jnp.concatenate([uk[:, 0], zk], axis=2),
                              jnp.concatenate([zk, uk[:, 1]], axis=2)], axis=1).astype(BF16)
    uv = jnp.transpose(c_w_uv, (1, 0, 2)).reshape(H_C // 2, 2, KV_LORA, DV)
    zv = jnp.zeros_like(uv[:, 0])
    wuv_bd = jnp.concatenate([jnp.concatenate([uv[:, 0], zv], axis=2),
                              jnp.concatenate([zv, uv[:, 1]], axis=2)], axis=1).astype(BF16)
    return dict(w_ab=w_ab, fb=fb, wo_ab=ab_w_out.astype(BF16), w_c=w_c, w_uq=w_uq, wuk_bd=wuk_bd,
                wuv_bd=wuv_bd, wo_c=c_w_out.astype(BF16))


def _pad_rows(x, rows):
    return jnp.pad(x, ((0, 0), (0, rows - x.shape[1]), (0, 0)))


def _trunk(x, mod0, mod1, caches, wts, norm_g, final_norm_g, c_q_norm_g, c_kv_norm_g, *, bt, tt, tq, tk, tq_c, tk_c):
    b, t, d = x.shape
    n = b * t
    past = 0 if caches is None else caches[0].shape[1]
    n_keys = past + t
    pos = past + jnp.arange(t)
    cos64, sin64 = _rope_tables(pos, DH, bt)
    cos32, sin32 = _rope_tables(pos, ROPE_DIM, bt)
    x2 = x.reshape(n, d)
    n_sel = min(TOPK_MAX, n_keys // 4)

    zb, ka, va, kb, vb, sm = _in_ab_call(x2, mod0, norm_g[0:1], wts["w_ab"], cos64, sin64, wts["fb"],
                                         t=t, bt=bt, tt=tt)
    zb3 = zb.reshape(b, t, ZW)
    sm3 = sm.reshape(b, t, LANES)
    ik_new = sm3[:, :, SM_IK:SM_IK + DH]
    logf_new = sm3[:, :, SM_FA:SM_FA + H_A]

    if caches is None:
        cum_t = _cumsum_call(logf_new.transpose(0, 2, 1))
        fq = cum_t.transpose(0, 2, 1)
        fk = cum_t.reshape(b, H_A, t // tk, tk).transpose(0, 2, 1, 3)
        mix_a = _fox_call(zb3, fq, fk, b=b, t=t, tq=tq, tk=tk)
        mix_b = _dsa_call(zb3, sm3, b=b, t=t, tq=tq, tk=tk, n_sel=n_sel)
    else:
        c_ak, c_av, c_alogf, c_bk, c_bv, c_bik = caches[:6]
        tk_total = past + NEW_BLOCK
        logf_all = _pad_rows(jnp.concatenate([c_alogf, logf_new], axis=1), tk_total)
        cum_t = _cumsum_call(logf_all.transpose(0, 2, 1))
        fq = cum_t[:, :, past:past + t].transpose(0, 2, 1)
        new = lambda c0, w: _pad_rows(zb3[:, :, c0:c0 + w], NEW_BLOCK)
        mix_a = _fox_s_call(zb3, new(KA, HW), new(VA, HW), c_ak.reshape(b, past, HW), c_av.reshape(b, past, HW),
                            fq, cum_t, b=b, t=t, past=past)
        ikc = jnp.concatenate([c_bik, c_bik], axis=-1).astype(BF16)
        bias = _dsa_sel_s_call(zb3, sm3, ikc, new(SM, LANES), b=b, t=t, past=past, n_sel=n_sel, bt=8)
        mix_b = _dsa_att_s_call(zb3, new(KB, HW), new(VB, HW), c_bk.reshape(b, past, HW),
                                c_bv.reshape(b, past, HW), bias, b=b, t=t, past=past)

    x1, g_c, qlat, qrope, kcat, clat, ckr = _mid_call(
        x2, mix_a.reshape(n, HW), mix_b.reshape(n, HW), wts["wo_ab"], mod0, mod1, norm_g[1:2], wts["w_c"],
        c_q_norm_g, c_kv_norm_g, wts["w_uq"], wts["wuk_bd"], cos32, sin32, t=t, bt=bt, tt=tt)

    kcat3 = kcat.reshape(b, t, 2 * LANES)
    qlat3 = qlat.reshape(b, t, -1)
    qrope3 = qrope.reshape(b, t, -1)
    if caches is None:
        olat = _mla_call(qlat3, qrope3, kcat3, b=b, t=t, tq=tq_c, tk=tk_c)
    else:
        c_lat, c_kr = caches[6:]
        krc = jnp.concatenate([c_kr] * (LANES // ROPE_DIM), axis=-1).astype(BF16)
        olat = _mla_s_call(qlat3, qrope3, c_lat, krc, _pad_rows(kcat3, NEW_BLOCK), b=b, t=t, past=past, bt=2)
    y = _out_c_call(x1, olat.reshape(n, -1), g_c, wts["wuv_bd"], wts["wo_c"], mod1, final_norm_g, t=t, bt=bt, tt=tt)

    state = (ka.reshape(1, b, t, H_A, DH), va.reshape(1, b, t, H_A, DH), logf_new[None],
             kb.reshape(1, b, t, H_B, DH), vb.reshape(1, b, t, H_B, DH), ik_new[None],
             clat.reshape(1, b, t, KV_LORA), ckr.reshape(b, t, LANES)[None, :, :, :ROPE_DIM])
    return y.reshape(b, t, d), state


def kernel(x_prompt, x_sample, c_prompt, c_sample, cache_a_k, cache_a_v, cache_a_logf, cache_b_k, cache_b_v, cache_b_idx_k, cache_c_latent, cache_c_krope, norm_g, ada_w, ada_b, final_norm_g, ab_w_in, ab_f_bias, ab_w_out, c_w_in, c_q_norm_g, c_kv_norm_g, c_w_uq, c_w_uk, c_w_uv, c_w_out):
    bp = x_prompt.shape[0]
    d = D_MODEL
    wts = _prep_weights(ab_w_in[0], ab_f_bias[0], ab_w_out[0], c_w_in[0], c_w_uq[0], c_w_uk[0], c_w_uv[0],
                        c_w_out[0])
    mod = _mod_call(jnp.concatenate([c_prompt, c_sample], axis=0), ada_w, ada_b)
    mod = mod.reshape(2, -1, 3, d)
    fg = final_norm_g.reshape(1, d)
    common = dict(wts=wts, norm_g=norm_g, final_norm_g=fg, c_q_norm_g=c_q_norm_g, c_kv_norm_g=c_kv_norm_g)

    y_p, st_p = _trunk(x_prompt, mod[0, :bp], mod[1, :bp], None, bt=1, tt=256, tq=512, tk=512, tq_c=128, tk_c=512,
                       **common)
    caches = (cache_a_k[0], cache_a_v[0], cache_a_logf[0], cache_b_k[0], cache_b_v[0], cache_b_idx_k[0],
              cache_c_latent[0], cache_c_krope[0])
    ts = x_sample.shape[1]
    y_s, st_s = _trunk(x_sample, mod[0, bp:], mod[1, bp:], caches, bt=256 // ts, tt=ts, tq=ts, tk=0, tq_c=ts,
                       tk_c=0, **common)
    return (y_p, y_s) + st_p + st_s
```
